```python
import jax, jax.numpy as jnp
from jax import lax
import numpy as np

D_MODEL = 1024
BATCH = 16
SEQ = 4096
DEPTH = 4
DEC_BATCH = 32
DEC_SEQ = 2048
PAST_LEN = 128

N_MIXERS = 3
N_A_LAYERS = (DEPTH + 2) // 3
N_B_LAYERS = (DEPTH + 1) // 3
N_C_LAYERS = DEPTH // 3
NORM_EPS = 1e-6
ROPE_THETA = 10000.0
NEG_INF = -1e30

A_HEADS = 16
A_KV_HEADS = 4
A_GROUP = A_HEADS // A_KV_HEADS
A_HEAD_DIM = 64
A_WINDOW = 128
A_BLOCK = 128
A_QKV_COLS = (A_HEADS + 2 * A_KV_HEADS) * A_HEAD_DIM

B_HEADS = 16
B_Q_LORA = 384
B_KV_LORA = 256
B_NOPE = 64
B_ROPE = 32
B_V = 64
B_QBLOCK = 128
B_IN_COLS = B_Q_LORA + B_KV_LORA + B_ROPE

GRID_W = 64
C_HEADS = 16
C_HEAD_DIM = 64
C_WIN_ROWS = 8
C_WIN_COLS = 16
C_QROWS = 2
C_KCOLS = 2 * C_WIN_COLS
C_NCB = GRID_W // C_WIN_COLS
C_QKV_COLS = 3 * C_HEADS * C_HEAD_DIM

FFN_DIM = 2816
CONV_WIDTH = 3

kernel_name = "hybrid_swa_mla_natten_convffn_encoder"


def rmsnorm(x, g):
    xf = x.astype(jnp.float32)
    y = xf * lax.rsqrt(jnp.mean(xf * xf, axis=-1, keepdims=True) + NORM_EPS)
    return (y * g.astype(jnp.float32)).astype(x.dtype)


def rope(x, pos):
    d = x.shape[-1]
    inv = ROPE_THETA ** (-jnp.arange(0, d, 2, dtype=jnp.float32) / d)
    ang = pos.astype(jnp.float32)[:, None] * inv[None, :]
    cos, sin = jnp.cos(ang)[:, None, :], jnp.sin(ang)[:, None, :]
    x1 = x[..., : d // 2].astype(jnp.float32)
    x2 = x[..., d // 2:].astype(jnp.float32)
    return jnp.concatenate([x1 * cos - x2 * sin, x1 * sin + x2 * cos], axis=-1).astype(x.dtype)


def window_attention(h, w_qkv, w_o, sink):
    B, T, _ = h.shape
    nb = T // A_BLOCK
    pos = jnp.arange(T)
    qkv = h @ w_qkv
    q = qkv[..., : A_HEADS * A_HEAD_DIM].reshape(B, T, A_HEADS, A_HEAD_DIM)
    k = qkv[..., A_HEADS * A_HEAD_DIM:(A_HEADS + A_KV_HEADS) * A_HEAD_DIM].reshape(B, T, A_KV_HEADS, A_HEAD_DIM)
    v = qkv[..., (A_HEADS + A_KV_HEADS) * A_HEAD_DIM:].reshape(B, T, A_KV_HEADS, A_HEAD_DIM)
    q = rope(q, pos).reshape(B, nb, A_BLOCK, A_KV_HEADS, A_GROUP, A_HEAD_DIM)
    k = rope(k, pos)
    pad = ((0, 0), (A_BLOCK, A_BLOCK), (0, 0), (0, 0))
    kp = jnp.pad(k, pad).reshape(B, nb + 2, A_BLOCK, A_KV_HEADS, A_HEAD_DIM)
    vp = jnp.pad(v, pad).reshape(B, nb + 2, A_BLOCK, A_KV_HEADS, A_HEAD_DIM)
    kb = jnp.concatenate([kp[:, :-2], kp[:, 1:-1], kp[:, 2:]], axis=2)
    vb = jnp.concatenate([vp[:, :-2], vp[:, 1:-1], vp[:, 2:]], axis=2)
    s = jnp.einsum('bnqhgd,bnkhd->bnhgqk', q, kb).astype(jnp.float32) * (A_HEAD_DIM ** -0.5)
    qpos = jnp.arange(nb)[:, None] * A_BLOCK + jnp.arange(A_BLOCK)[None, :]
    kpos = jnp.arange(nb)[:, None] * A_BLOCK - A_BLOCK + jnp.arange(3 * A_BLOCK)[None, :]
    mask = (jnp.abs(qpos[:, :, None] - kpos[:, None, :]) <= A_WINDOW) & ((kpos >= 0) & (kpos < T))[:, None, :]
    s = jnp.where(mask[None, :, None, None], s, NEG_INF)
    sl = sink.astype(jnp.float32).reshape(A_KV_HEADS, A_GROUP)[None, None, :, :, None, None]
    m = jnp.maximum(jnp.max(s, axis=-1, keepdims=True), sl)
    p = jnp.exp(s - m)
    p = p / (jnp.sum(p, axis=-1, keepdims=True) + jnp.exp(sl - m))
    o = jnp.einsum('bnhgqk,bnkhd->bnqhgd', p.astype(vb.dtype), vb)
    return o.reshape(B, T, A_HEADS * A_HEAD_DIM) @ w_o


def latent_attention(h, w_in, q_norm, kv_norm, w_uq, w_ukv, w_o):
    B, T, _ = h.shape
    nqb = T // B_QBLOCK
    pos = jnp.arange(T)
    c = h @ w_in
    cq = rmsnorm(c[..., :B_Q_LORA], q_norm)
    ckv = rmsnorm(c[..., B_Q_LORA:B_Q_LORA + B_KV_LORA], kv_norm)
    kr = rope(c[..., B_Q_LORA + B_KV_LORA:][:, :, None, :], pos)[:, :, 0]
    q = (cq @ w_uq).reshape(B, T, B_HEADS, B_NOPE + B_ROPE)
    q_nope = q[..., :B_NOPE]
    q_rope = rope(q[..., B_NOPE:], pos)
    kv = (ckv @ w_ukv).reshape(B, T, B_HEADS, B_NOPE + B_V)
    k_nope, v = kv[..., :B_NOPE], kv[..., B_NOPE:]
    scale = (B_NOPE + B_ROPE) ** -0.5
    qn = q_nope.reshape(B, nqb, B_QBLOCK, B_HEADS, B_NOPE).transpose(1, 0, 2, 3, 4)
    qr = q_rope.reshape(B, nqb, B_QBLOCK, B_HEADS, B_ROPE).transpose(1, 0, 2, 3, 4)

    def block(args):
        qn_b, qr_b = args
        s = (jnp.einsum('bqhd,bkhd->bhqk', qn_b, k_nope)
             + jnp.einsum('bqhd,bkd->bhqk', qr_b, kr)).astype(jnp.float32) * scale
        p = jax.nn.softmax(s, axis=-1)
        return jnp.einsum('bhqk,bkhd->bqhd', p.astype(v.dtype), v)

    o = lax.map(block, (qn, qr))
    return o.transpose(1, 0, 2, 3, 4).reshape(B, T, B_HEADS * B_V) @ w_o


def neighbourhood_attention(h, w_qkv, rpb, w_o):
    B, T, _ = h.shape
    rows = T // GRID_W
    wr = min(C_WIN_ROWS, rows)
    kr_n = min(wr + 1, rows)
    nrb = rows // C_QROWS
    qkv = (h @ w_qkv).reshape(B, rows, GRID_W, 3, C_HEADS, C_HEAD_DIM)
    q = qkv[:, :, :, 0] * (C_HEAD_DIM ** -0.5)
    k = qkv[:, :, :, 1]
    v = qkv[:, :, :, 2]
    kc0 = np.clip(np.arange(C_NCB) * C_WIN_COLS - C_WIN_COLS // 2, 0, GRID_W - C_KCOLS)
    kcols = kc0[:, None] + np.arange(C_KCOLS)[None, :]
    qcol = np.arange(GRID_W).reshape(C_NCB, C_WIN_COLS)
    cs = np.clip(qcol - C_WIN_COLS // 2, 0, GRID_W - C_WIN_COLS)
    col_ok = (kcols[:, None, :] >= cs[:, :, None]) & (kcols[:, None, :] < cs[:, :, None] + C_WIN_COLS)
    dci = np.clip(kcols[:, None, :] - qcol[:, :, None] + C_WIN_COLS - 1, 0, 2 * C_WIN_COLS - 2)

    def block(rb):
        r0 = rb * C_QROWS
        kr0 = jnp.clip(r0 - wr // 2, 0, rows - kr_n)
        q_b = lax.dynamic_slice_in_dim(q, r0, C_QROWS, axis=1).reshape(
            B, C_QROWS, C_NCB, C_WIN_COLS, C_HEADS, C_HEAD_DIM)
        k_b = lax.dynamic_slice_in_dim(k, kr0, kr_n, axis=1)[:, :, kcols]
        v_b = lax.dynamic_slice_in_dim(v, kr0, kr_n, axis=1)[:, :, kcols]
        qrow = r0 + jnp.arange(C_QROWS)
        krow = kr0 + jnp.arange(kr_n)
        rs = jnp.clip(qrow - wr // 2, 0, rows - wr)
        row_ok = (krow[None, :] >= rs[:, None]) & (krow[None, :] < rs[:, None] + wr)
        dri = jnp.clip(krow[None, :] - qrow[:, None] + C_WIN_ROWS - 1, 0, 2 * C_WIN_ROWS - 2)
        bias = rpb.astype(jnp.float32)[:, dri[None, :, None, :, None], dci[:, None, :, None, :]]
        mask = row_ok[None, :, None, :, None] & col_ok[:, None, :, None, :]
        s = jnp.einsum('brjchd,bkjlhd->bhjrckl', q_b, k_b).astype(jnp.float32) + bias[None]
        s = jnp.where(mask[None, None], s, NEG_INF)
        sh = s.shape
        p = jax.nn.softmax(s.reshape(sh[:-2] + (sh[-2] * sh[-1],)), axis=-1).reshape(sh)
        o = jnp.einsum('bhjrckl,bkjlhd->brjchd', p.astype(v_b.dtype), v_b)
        return o.reshape(B, C_QROWS, GRID_W, C_HEADS * C_HEAD_DIM)

    o = lax.map(block, jnp.arange(nrb))
    return o.transpose(1, 0, 2, 3, 4).reshape(B, T, C_HEADS * C_HEAD_DIM) @ w_o


def conv_ffn(h, w_in, conv_w, conv_b, w_out):
    T = h.shape[1]
    u = h @ w_in
    up = jnp.pad(u, ((0, 0), (1, 1), (0, 0)))
    u = conv_b + up[:, :T] * conv_w[0] + up[:, 1:T + 1] * conv_w[1] + up[:, 2:T + 2] * conv_w[2]
    g, val = u[..., :FFN_DIM], u[..., FFN_DIM:]
    return (jax.nn.silu(g) * val) @ w_out


def trunk(x, norm_mix, norm_ffn, norm_final, a_w_qkv, a_w_o, a_sink,
          b_w_in, b_q_norm, b_kv_norm, b_w_uq, b_w_ukv, b_w_o,
          c_w_qkv, c_rpb, c_w_o, f_w_in, f_conv_w, f_conv_b, f_w_out):
    for i in range(DEPTH):
        kind, j = i % N_MIXERS, i // N_MIXERS
        h = rmsnorm(x, norm_mix[i])
        if kind == 0:
            x = x + window_attention(h, a_w_qkv[j], a_w_o[j], a_sink[j])
        elif kind == 1:
            x = x + latent_attention(h, b_w_in[j], b_q_norm[j], b_kv_norm[j], b_w_uq[j], b_w_ukv[j], b_w_o[j])
        else:
            x = x + neighbourhood_attention(h, c_w_qkv[j], c_rpb[j], c_w_o[j])
        x = x + conv_ffn(rmsnorm(x, norm_ffn[i]), f_w_in[i], f_conv_w[i], f_conv_b[i], f_w_out[i])
    return rmsnorm(x, norm_final)


def _dense(key, shape, fan_in):
    return jax.random.normal(key, shape, jnp.float32) * fan_in ** -0.5


def _gain(key, shape):
    return 1.0 + 0.1 * jax.random.normal(key, shape, jnp.float32)


def setup_inputs(seed: int = 0) -> dict:
    key = jax.random.key(seed)
    ks = jax.random.split(key, 21)
    return {
        "x_prompt": jax.random.normal(ks[0], (BATCH, SEQ, D_MODEL), jnp.float32),
        "x_sample": jax.random.normal(ks[1], (DEC_BATCH, DEC_SEQ, D_MODEL), jnp.float32),
        "norm_mix": _gain(ks[2], (DEPTH, D_MODEL)),
        "norm_ffn": _gain(ks[3], (DEPTH, D_MODEL)),
        "norm_final": _gain(ks[4], (D_MODEL,)),
        "a_w_qkv": _dense(ks[5], (N_A_LAYERS, D_MODEL, A_QKV_COLS), D_MODEL),
        "a_w_o": _dense(ks[6], (N_A_LAYERS, A_HEADS * A_HEAD_DIM, D_MODEL), A_HEADS * A_HEAD_DIM),
        "a_sink": 0.5 * jax.random.normal(ks[7], (N_A_LAYERS, A_HEADS), jnp.float32),
        "b_w_in": _dense(ks[8], (N_B_LAYERS, D_MODEL, B_IN_COLS), D_MODEL),
        "b_q_norm": _gain(ks[9], (N_B_LAYERS, B_Q_LORA)),
        "b_kv_norm": _gain(ks[10], (N_B_LAYERS, B_KV_LORA)),
        "b_w_uq": _dense(ks[11], (N_B_LAYERS, B_Q_LORA, B_HEADS * (B_NOPE + B_ROPE)), B_Q_LORA),
        "b_w_ukv": _dense(ks[12], (N_B_LAYERS, B_KV_LORA, B_HEADS * (B_NOPE + B_V)), B_KV_LORA),
        "b_w_o": _dense(ks[13], (N_B_LAYERS, B_HEADS * B_V, D_MODEL), B_HEADS * B_V),
        "c_w_qkv": _dense(ks[14], (N_C_LAYERS, D_MODEL, C_QKV_COLS), D_MODEL),
        "c_rpb": 0.5 * jax.random.normal(ks[15], (N_C_LAYERS, C_HEADS, 2 * C_WIN_ROWS - 1, 2 * C_WIN_COLS - 1), jnp.float32),
        "c_w_o": _dense(ks[16], (N_C_LAYERS, C_HEADS * C_HEAD_DIM, D_MODEL), C_HEADS * C_HEAD_DIM),
        "f_w_in": _dense(ks[17], (DEPTH, D_MODEL, 2 * FFN_DIM), D_MODEL),
        "f_conv_w": _dense(ks[18], (DEPTH, CONV_WIDTH, 2 * FFN_DIM), CONV_WIDTH),
        "f_conv_b": 0.01 * jax.random.normal(ks[19], (DEPTH, 2 * FFN_DIM), jnp.float32),
        "f_w_out": _dense(ks[20], (DEPTH, FFN_DIM, D_MODEL), FFN_DIM),
    }


def reference(x_prompt, x_sample, norm_mix, norm_ffn, norm_final, a_w_qkv, a_w_o, a_sink,
              b_w_in, b_q_norm, b_kv_norm, b_w_uq, b_w_ukv, b_w_o,
              c_w_qkv, c_rpb, c_w_o, f_w_in, f_conv_w, f_conv_b, f_w_out):
    y_prompt = trunk(x_prompt, norm_mix, norm_ffn, norm_final, a_w_qkv, a_w_o, a_sink,
                     b_w_in, b_q_norm, b_kv_norm, b_w_uq, b_w_ukv, b_w_o,
                     c_w_qkv, c_rpb, c_w_o, f_w_in, f_conv_w, f_conv_b, f_w_out)
    y_sample = trunk(x_sample, norm_mix, norm_ffn, norm_final, a_w_qkv, a_w_o, a_sink,
                     b_w_in, b_q_norm, b_kv_norm, b_w_uq, b_w_ukv, b_w_o,
                     c_w_qkv, c_rpb, c_w_o, f_w_in, f_conv_w, f_conv_b, f_w_out)
    return (y_prompt, y_sample)
```

```python
import functools

import numpy as np
import jax
import jax.numpy as jnp
from jax import lax
from jax.experimental import pallas as pl
from jax.experimental.pallas import tpu as pltpu

F32 = jnp.float32
BF16 = jnp.bfloat16

D_MODEL = 1024
NORM_EPS = 1e-6
ROPE_THETA = 10000.0
NEG_INF = -1e30

LANES = 128
BF16_ROWS = 16
VMEM_LIMIT = 52 * 1024 * 1024

A_HEADS, A_KV_HEADS, A_HEAD_DIM, A_WINDOW = 16, 4, 64, 128
A_GROUP = A_HEADS // A_KV_HEADS
A_Q_COLS = A_HEADS * A_HEAD_DIM
A_KV_COLS = A_KV_HEADS * A_HEAD_DIM
B_HEADS, B_Q_LORA, B_KV_LORA, B_NOPE, B_ROPE, B_V = 16, 384, 256, 64, 32, 64
B_IN_PAD = B_Q_LORA + B_KV_LORA + LANES
GRID_W, C_HEADS, C_HEAD_DIM, C_WIN_ROWS, C_WIN_COLS = 64, 16, 64, 8, 16
C_QROWS = 2
C_KROWS = 10
FFN_DIM = 2816
FFN_CHUNK = 256


def _cparams(n_axes):
    return pltpu.CompilerParams(dimension_semantics=("arbitrary",) * n_axes, vmem_limit_bytes=VMEM_LIMIT)


def _resident(shape):
    zeros = (0,) * len(shape)
    return pl.BlockSpec(shape, lambda *_: zeros, pipeline_mode=pl.Buffered(1))


def _rms(x, g):
    return x * lax.rsqrt(jnp.mean(x * x, axis=-1, keepdims=True) + NORM_EPS) * g


def _low_half():
    return lax.broadcasted_iota(jnp.int32, (1, LANES), 1) < (LANES // 2)


def _split_pair(x, lo):
    zero = jnp.zeros_like(x)
    return jnp.concatenate([jnp.where(lo, x, zero), jnp.where(lo, zero, x)], axis=0)


def _dot_nt(a, b):
    return lax.dot_general(a, b, (((1,), (1,)), ((), ())), preferred_element_type=F32)


def _proj_kernel(*refs, n_out, rope_cols, chunk):
    if rope_cols:
        x_ref, g_ref, w_ref, cos_ref, sin_ref, o_ref = refs
    else:
        x_ref, g_ref, w_ref, o_ref = refs
    h = _rms(x_ref[...], g_ref[...]).astype(BF16)
    if rope_cols:
        lane = lax.broadcasted_iota(jnp.int32, (1, LANES), 1)
        first = (lane % A_HEAD_DIM) < (A_HEAD_DIM // 2)
        cos = cos_ref[...]
        sin = sin_ref[...]
    for c0 in range(0, n_out, chunk):
        y = jnp.dot(h, w_ref[:, c0:c0 + chunk], preferred_element_type=F32)
        for s in range(0, chunk, LANES):
            yc = y[:, s:s + LANES]
            if c0 + s < rope_cols:
                rot = jnp.where(first, pltpu.roll(yc, LANES - A_HEAD_DIM // 2, 1),
                                pltpu.roll(yc, A_HEAD_DIM // 2, 1))
                yc = yc * cos + rot * sin
            o_ref[:, c0 + s:c0 + s + LANES] = yc.astype(BF16)


def _norm_proj(x, gain, w, seq_len, rope=None, rope_cols=0, tm=512, chunk=512):
    n, d = x.shape
    n_out = w.shape[1]
    assert n % tm == 0 and seq_len % tm == 0 and n_out % chunk == 0
    tiles_per_seq = seq_len // tm
    in_specs = [pl.BlockSpec((tm, d), lambda i: (i, 0)), _resident((1, d)), _resident((d, n_out))]
    args = [x, gain.reshape(1, d), w]
    if rope_cols:
        in_specs += [pl.BlockSpec((tm, LANES), lambda i: (i % tiles_per_seq, 0))] * 2
        args += list(rope)
    return pl.pallas_call(
        functools.partial(_proj_kernel, n_out=n_out, rope_cols=rope_cols, chunk=chunk),
        out_shape=jax.ShapeDtypeStruct((n, n_out), BF16),
        grid=(n // tm,),
        in_specs=in_specs,
        out_specs=pl.BlockSpec((tm, n_out), lambda i: (i, 0)),
        compiler_params=_cparams(1),
    )(*args)


def _oproj_kernel(x_ref, o_ref, w_ref, y_ref):
    y_ref[...] = x_ref[...] + jnp.dot(o_ref[...], w_ref[...], preferred_element_type=F32)


def _oproj(x, o, w, tm=512):
    n, d = x.shape
    k = o.shape[1]
    assert n % tm == 0
    return pl.pallas_call(
        _oproj_kernel,
        out_shape=jax.ShapeDtypeStruct((n, d), F32),
        grid=(n // tm,),
        in_specs=[pl.BlockSpec((tm, d), lambda i: (i, 0)), pl.BlockSpec((tm, k), lambda i: (i, 0)),
                  _resident((k, d))],
        out_specs=pl.BlockSpec((tm, d), lambda i: (i, 0)),
        compiler_params=_cparams(1),
    )(x, o, w)


def _attn_a_kernel(sink_ref, q_ref, kp_ref, km_ref, kn_ref, vp_ref, vm_ref, vn_ref, band_ref, o_ref,
                   kbuf, vbuf, *, tq):
    w = A_WINDOW
    i = pl.program_id(1)
    n_i = pl.num_programs(1)
    kbuf[0:w] = kp_ref[...]
    kbuf[w:w + tq] = km_ref[...]
    kbuf[w + tq:] = kn_ref[...]
    vbuf[0:w] = vp_ref[...]
    vbuf[w:w + tq] = vm_ref[...]
    vbuf[w + tq:] = vn_ref[...]
    lo = _low_half()
    kidx = lax.broadcasted_iota(jnp.int32, (1, 3 * w), 1)
    band = band_ref[...]
    n_sub = tq // w
    for j in range(n_sub):
        bias = band
        if j == 0:
            bias = bias + jnp.where(kidx < w, jnp.where(i == 0, NEG_INF, 0.0), 0.0)
        if j == n_sub - 1:
            bias = bias + jnp.where(kidx >= 2 * w, jnp.where(i == n_i - 1, NEG_INF, 0.0), 0.0)
        for p in range(A_KV_HEADS // 2):
            kw = kbuf[j * w:(j + 3) * w, p * LANES:(p + 1) * LANES]
            vw = vbuf[j * w:(j + 3) * w, p * LANES:(p + 1) * LANES]
            qg = jnp.concatenate(
                [q_ref[j * w:(j + 1) * w, (p * A_GROUP + g) * LANES:(p * A_GROUP + g + 1) * LANES]
                 for g in range(A_GROUP)], axis=0)
            s = _dot_nt(_split_pair(qg, lo), kw)
            probs, invs = [], []
            for e in range(2):
                for g in range(A_GROUP):
                    r0 = (e * A_GROUP + g) * w
                    sh = s[r0:r0 + w] + bias
                    sk = sink_ref[(2 * p + e) * A_GROUP + g]
                    m = jnp.maximum(jnp.max(sh, axis=-1, keepdims=True), sk)
                    pe = jnp.exp(sh - m)
                    invs.append(1.0 / (jnp.sum(pe, axis=-1, keepdims=True) + jnp.exp(sk - m)))
                    probs.append(pe.astype(BF16))
            r = jnp.dot(jnp.concatenate(probs, axis=0), vw, preferred_element_type=F32)
            for g in range(A_GROUP):
                o = jnp.where(lo, r[g * w:(g + 1) * w] * invs[g],
                              r[(A_GROUP + g) * w:(A_GROUP + g + 1) * w] * invs[A_GROUP + g])
                c = (p * A_GROUP + g) * LANES
                o_ref[j * w:(j + 1) * w, c:c + LANES] = o.astype(BF16)


def _attn_a(qkv, sink, seq_len, tq=512):
    n = qkv.shape[0]
    w = A_WINDOW
    tq = min(tq, seq_len)
    assert seq_len % tq == 0 and tq % w == 0 and n % seq_len == 0
    nq, nb, r = seq_len // tq, seq_len // w, tq // w
    kcol, vcol = A_Q_COLS // A_KV_COLS, A_Q_COLS // A_KV_COLS + 1
    rel = np.arange(3 * w)[None, :] - w - np.arange(w)[:, None]
    band = jnp.asarray(np.where(np.abs(rel) <= w, 0.0, NEG_INF), F32)

    def main(col):
        return pl.BlockSpec((tq, A_KV_COLS), lambda b, i: (b * nq + i, col))

    def prev(col):
        return pl.BlockSpec((w, A_KV_COLS), lambda b, i: (b * nb + jnp.maximum(i * r - 1, 0), col))

    def nxt(col):
        return pl.BlockSpec((w, A_KV_COLS), lambda b, i: (b * nb + jnp.minimum((i + 1) * r, nb - 1), col))

    return pl.pallas_call(
        functools.partial(_attn_a_kernel, tq=tq),
        out_shape=jax.ShapeDtypeStruct((n, A_Q_COLS), BF16),
        grid=(n // seq_len, nq),
        in_specs=[pl.BlockSpec(memory_space=pltpu.SMEM),
                  pl.BlockSpec((tq, A_Q_COLS), lambda b, i: (b * nq + i, 0)),
                  prev(kcol), main(kcol), nxt(kcol), prev(vcol), main(vcol), nxt(vcol),
                  _resident((w, 3 * w))],
        out_specs=pl.BlockSpec((tq, A_Q_COLS), lambda b, i: (b * nq + i, 0)),
        scratch_shapes=[pltpu.VMEM((tq + 2 * w, A_KV_COLS), BF16), pltpu.VMEM((tq + 2 * w, A_KV_COLS), BF16)],
        compiler_params=_cparams(2),
    )(sink, qkv, qkv, qkv, qkv, qkv, qkv, qkv, band)


def _proj_b_kernel(x_ref, g_ref, win_ref, qn_ref, kvn_ref, wuq_ref, wukv_ref, cos_ref, sa_ref, sb_ref,
                   q_ref, k_ref, v_ref):
    h = _rms(x_ref[...], g_ref[...]).astype(BF16)
    c = jnp.dot(h, win_ref[...], preferred_element_type=F32)
    cq = _rms(c[:, :B_Q_LORA], qn_ref[...]).astype(BF16)
    ckv = _rms(c[:, B_Q_LORA:B_Q_LORA + B_KV_LORA], kvn_ref[...]).astype(BF16)
    cos, sa, sb = cos_ref[...], sa_ref[...], sb_ref[...]
    half = B_ROPE // 2

    def rope(y):
        return y * cos + pltpu.roll(y, LANES - half, 1) * sa + pltpu.roll(y, half, 1) * sb

    kr = rope(c[:, B_Q_LORA + B_KV_LORA:])
    scale = (B_NOPE + B_ROPE) ** -0.5
    for hh in range(B_HEADS):
        sl = slice(hh * LANES, (hh + 1) * LANES)
        q = jnp.dot(cq, wuq_ref[:, sl], preferred_element_type=F32)
        q_ref[:, sl] = (rope(q) * scale).astype(BF16)
        k = jnp.dot(ckv, wukv_ref[:, sl], preferred_element_type=F32)
        k_ref[:, sl] = (k + kr).astype(BF16)
    nk = B_HEADS * LANES
    v_ref[...] = jnp.dot(ckv, wukv_ref[:, nk:], preferred_element_type=F32).astype(BF16)


def _proj_b(x, gain, w_in, q_norm, kv_norm, w_uq, w_ukv, tables, seq_len, tm=512):
    n, d = x.shape
    assert n % tm == 0 and seq_len % tm == 0
    tps = seq_len // tm
    nq = B_HEADS * LANES
    nv = B_HEADS * B_V
    tab = pl.BlockSpec((tm, LANES), lambda i: (i % tps, 0))
    row = lambda c: pl.BlockSpec((tm, c), lambda i: (i, 0))
    return pl.pallas_call(
        _proj_b_kernel,
        out_shape=(jax.ShapeDtypeStruct((n, nq), BF16), jax.ShapeDtypeStruct((n, nq), BF16),
                   jax.ShapeDtypeStruct((n, nv), BF16)),
        grid=(n // tm,),
        in_specs=[row(d), _resident((1, d)), _resident((d, B_IN_PAD)), _resident((1, B_Q_LORA)),
                  _resident((1, B_KV_LORA)), _resident((B_Q_LORA, nq)), _resident((B_KV_LORA, nq + nv)),
                  tab, tab, tab],
        out_specs=(row(nq), row(nq), row(nv)),
        compiler_params=_cparams(1),
    )(x, gain.reshape(1, d), w_in, q_norm.reshape(1, -1), kv_norm.reshape(1, -1), w_uq, w_ukv, *tables)


def _attn_b_kernel(q_ref, k_ref, v_ref, o_ref, *, tq, tk, n_k):
    lo = _low_half()
    q0, q1 = q_ref[:, :LANES], q_ref[:, LANES:]

    def update(q, kc, m_old, l_old):
        s = _dot_nt(q, kc)
        m_new = jnp.maximum(m_old, jnp.max(s, axis=-1, keepdims=True))
        alpha = jnp.exp(m_old - m_new)
        pe = jnp.exp(s - m_new)
        return m_new, alpha * l_old + jnp.sum(pe, axis=-1, keepdims=True), alpha, pe.astype(BF16)

    def body(j, carry):
        m0, m1, l0, l1, acc = carry
        r0 = pl.multiple_of(j * tk, tk)
        kc = k_ref[pl.ds(r0, tk), :]
        m0, l0, a0, p0 = update(q0, kc[:, :LANES], m0, l0)
        m1, l1, a1, p1 = update(q1, kc[:, LANES:], m1, l1)
        r = jnp.dot(jnp.concatenate([p0, p1], axis=0), v_ref[pl.ds(r0, tk), :], preferred_element_type=F32)
        acc = acc * jnp.where(lo, a0, a1) + jnp.where(lo, r[:tq], r[tq:])
        return m0, m1, l0, l1, acc

    col = lambda v: jnp.full((tq, 1), v, F32)
    init = (col(NEG_INF), col(NEG_INF), col(0.0), col(0.0), jnp.zeros((tq, LANES), F32))
    _, _, l0, l1, acc = lax.fori_loop(0, n_k, body, init, unroll=2)
    o_ref[...] = (acc * jnp.where(lo, 1.0 / l0, 1.0 / l1)).astype(BF16)


def _attn_b(q, k, v, seq_len, tq=512, tk=512):
    n = q.shape[0]
    tq, tk = min(tq, seq_len), min(tk, seq_len)
    assert seq_len % tq == 0 and seq_len % tk == 0 and n % seq_len == 0 and (seq_len // tk) % 2 == 0
    nq = seq_len // tq
    return pl.pallas_call(
        functools.partial(_attn_b_kernel, tq=tq, tk=tk, n_k=seq_len // tk),
        out_shape=jax.ShapeDtypeStruct((n, v.shape[1]), BF16),
        grid=(n // seq_len, B_HEADS // 2, nq),
        in_specs=[pl.BlockSpec((tq, 2 * LANES), lambda b, p, i: (b * nq + i, p)),
                  pl.BlockSpec((seq_len, 2 * LANES), lambda b, p, i: (b, p)),
                  pl.BlockSpec((seq_len, LANES), lambda b, p, i: (b, p))],
        out_specs=pl.BlockSpec((tq, LANES), lambda b, p, i: (b * nq + i, p)),
        compiler_params=_cparams(3),
    )(q, k, v)


def _attn_c_kernel(q_ref, k0, k1, k2, k3, k4, v0, v1, v2, v3, v4, tab_ref, o_ref):
    lo = _low_half()
    nq = q_ref.shape[0]
    for p in range(C_HEADS // 2):
        ps = slice(p * LANES, (p + 1) * LANES)
        kc = jnp.concatenate([r[:, ps] for r in (k0, k1, k2, k3, k4)], axis=0)
        vc = jnp.concatenate([r[:, ps] for r in (v0, v1, v2, v3, v4)], axis=0)
        s = _dot_nt(_split_pair(q_ref[:, ps], lo), kc)
        probs, invs = [], []
        for e in range(2):
            sh = s[e * nq:(e + 1) * nq] + tab_ref[0, 2 * p + e]
            m = jnp.max(sh, axis=-1, keepdims=True)
            pe = jnp.exp(sh - m)
            invs.append(1.0 / jnp.sum(pe, axis=-1, keepdims=True))
            probs.append(pe.astype(BF16))
        r = jnp.dot(jnp.concatenate(probs, axis=0), vc, preferred_element_type=F32)
        o_ref[:, ps] = jnp.where(lo, r[:nq] * invs[0], r[nq:] * invs[1]).astype(BF16)


def _c_window_start(rb, rows):
    return jnp.clip(rb - 2, 0, rows // 2 - C_KROWS // 2)


def _c_bias_tables(rpb, rows):
    nrb = rows // C_QROWS
    qcol = np.arange(GRID_W)
    kcol = np.arange(GRID_W)
    cs = np.clip(qcol - C_WIN_COLS // 2, 0, GRID_W - C_WIN_COLS)
    col_ok = (kcol[None, :] >= cs[:, None]) & (kcol[None, :] < cs[:, None] + C_WIN_COLS)
    dci = np.clip(kcol[None, :] - qcol[:, None] + C_WIN_COLS - 1, 0, 2 * C_WIN_COLS - 2)
    tabs = []
    for rb in (0, 1, 2, nrb - 2, nrb - 1):
        w0 = 2 * int(np.clip(rb - 2, 0, rows // 2 - C_KROWS // 2))
        qrow = rb * C_QROWS + np.arange(C_QROWS)
        krow = w0 + np.arange(C_KROWS)
        rs = np.clip(qrow - C_WIN_ROWS // 2, 0, rows - C_WIN_ROWS)
        row_ok = (krow[None, :] >= rs[:, None]) & (krow[None, :] < rs[:, None] + C_WIN_ROWS)
        dri = np.clip(krow[None, :] - qrow[:, None] + C_WIN_ROWS - 1, 0, 2 * C_WIN_ROWS - 2)
        bias = rpb[:, dri[:, None, :, None], dci[None, :, None, :]]
        ok = row_ok[:, None, :, None] & col_ok[None, :, None, :]
        bias = jnp.where(jnp.asarray(ok)[None], bias, NEG_INF)
        tabs.append(bias.reshape(C_HEADS, C_QROWS * GRID_W, C_KROWS * GRID_W))
    return jnp.stack(tabs)


def _attn_c(qkv, rpb, seq_len):
    n = qkv.shape[0]
    rows = seq_len // GRID_W
    nrb = rows // C_QROWS
    assert seq_len % GRID_W == 0 and rows % 2 == 0 and rows >= C_KROWS + 2 and nrb >= 6
    tq = C_QROWS * GRID_W
    nkb = C_KROWS * GRID_W // tq
    nh = C_HEADS * C_HEAD_DIM
    tables = _c_bias_tables(rpb.astype(F32), rows)
    blocks_per_seq = seq_len // tq

    def kv(col, j):
        return pl.BlockSpec((tq, nh), lambda rb, b: (b * blocks_per_seq + _c_window_start(rb, rows) + j, col))

    def pattern(rb, b):
        pat = jnp.where(rb < 2, rb, jnp.where(rb >= nrb - 2, rb - (nrb - 2) + 3, 2))
        return (pat, 0, 0, 0)

    return pl.pallas_call(
        _attn_c_kernel,
        out_shape=jax.ShapeDtypeStruct((n, nh), BF16),
        grid=(nrb, n // seq_len),
        in_specs=[pl.BlockSpec((tq, nh), lambda rb, b: (b * blocks_per_seq + rb, 0))]
                 + [kv(1, j) for j in range(nkb)] + [kv(2, j) for j in range(nkb)]
                 + [pl.BlockSpec((1, C_HEADS, tq, C_KROWS * GRID_W), pattern)],
        out_specs=pl.BlockSpec((tq, nh), lambda rb, b: (b * blocks_per_seq + rb, 0)),
        compiler_params=_cparams(2),
    )(*([qkv] * (1 + 2 * nkb)), tables)


def _ffn_kernel(*refs, tm, n_chunks, fc, tiles_per_seq, final):
    if final:
        xm_ref, xp_ref, xn_ref, g_ref, win_ref, cw_ref, wout_ref, fg_ref, o_ref, h_scr = refs
    else:
        xm_ref, xp_ref, xn_ref, g_ref, win_ref, cw_ref, wout_ref, o_ref, h_scr = refs
    t = pl.program_id(0) % tiles_per_seq
    g = g_ref[...]
    xm = xm_ref[...]
    h_scr[0:tm] = _rms(xm, g).astype(BF16)
    hp = jnp.where(t == 0, 0.0, _rms(xp_ref[...], g))
    hn = jnp.where(t == tiles_per_seq - 1, 0.0, _rms(xn_ref[...], g))
    row = lax.broadcasted_iota(jnp.int32, (BF16_ROWS, 1), 0)
    h_scr[tm:] = jnp.where(row < BF16_ROWS // 2, hn, hp).astype(BF16)
    hext = h_scr[...]
    m = tm + BF16_ROWS
    acc = xm
    for c in range(n_chunks):
        u = jnp.dot(hext, win_ref[c], preferred_element_type=F32)
        cw = cw_ref[c]
        z = (cw[3:4] + pltpu.roll(u, 1, 0)[:tm] * cw[0:1] + u[:tm] * cw[1:2]
             + pltpu.roll(u, m - 1, 0)[:tm] * cw[2:3])
        gate, val = z[:, :fc], z[:, fc:]
        a = (gate * (1.0 / (1.0 + jnp.exp(-gate))) * val).astype(BF16)
        acc = acc + jnp.dot(a, wout_ref[c], preferred_element_type=F32)
    if final:
        acc = _rms(acc, fg_ref[...])
    o_ref[...] = acc


def _ffn(x, gain, w_in, cw, w_out, seq_len, final_gain=None, tm=512):
    n, d = x.shape
    n_chunks, _, fc2 = w_in.shape
    assert n % tm == 0 and seq_len % tm == 0 and tm % BF16_ROWS == 0
    tps = seq_len // tm
    r = tm // BF16_ROWS
    nblk = n // BF16_ROWS
    final = final_gain is not None
    in_specs = [pl.BlockSpec((tm, d), lambda i: (i, 0)),
                pl.BlockSpec((BF16_ROWS, d), lambda i: (jnp.maximum(i * r - 1, 0), 0)),
                pl.BlockSpec((BF16_ROWS, d), lambda i: (jnp.minimum((i + 1) * r, nblk - 1), 0)),
                _resident((1, d)), _resident(w_in.shape), _resident(cw.shape), _resident(w_out.shape)]
    args = [x, x, x, gain.reshape(1, d), w_in, cw, w_out]
    if final:
        in_specs.append(_resident((1, d)))
        args.append(final_gain.reshape(1, d))
    return pl.pallas_call(
        functools.partial(_ffn_kernel, tm=tm, n_chunks=n_chunks, fc=fc2 // 2, tiles_per_seq=tps, final=final),
        out_shape=jax.ShapeDtypeStruct((n, d), F32),
        grid=(n // tm,),
        in_specs=in_specs,
        out_specs=pl.BlockSpec((tm, d), lambda i: (i, 0)),
        scratch_shapes=[pltpu.VMEM((tm + BF16_ROWS, d), BF16)],
        compiler_params=_cparams(1),
    )(*args)


def _a_head_order():
    order = []
    for p in range(A_KV_HEADS // 2):
        for g in range(A_GROUP):
            order += [(2 * p) * A_GROUP + g, (2 * p + 1) * A_GROUP + g]
    return np.asarray(order)


def _prep_a(w_qkv, w_o):
    cols = (_a_head_order()[:, None] * A_HEAD_DIM + np.arange(A_HEAD_DIM)[None, :]).reshape(-1)
    wq = w_qkv[:, :A_Q_COLS][:, cols] * (A_HEAD_DIM ** -0.5)
    w = jnp.concatenate([wq, w_qkv[:, A_Q_COLS:]], axis=1).astype(BF16)
    return w, w_o[cols, :].astype(BF16)


def _prep_b(w_in, w_uq, w_ukv):
    d = w_in.shape[0]
    nl = B_Q_LORA + B_KV_LORA
    win = jnp.concatenate([w_in[:, :nl], jnp.zeros((d, B_NOPE), F32), w_in[:, nl:],
                           jnp.zeros((d, LANES - B_NOPE - B_ROPE), F32)], axis=1).astype(BF16)
    uq = w_uq.reshape(B_Q_LORA, B_HEADS, B_NOPE + B_ROPE)
    uq = jnp.pad(uq, ((0, 0), (0, 0), (0, LANES - B_NOPE - B_ROPE))).reshape(B_Q_LORA, B_HEADS * LANES)
    ukv = w_ukv.reshape(B_KV_LORA, B_HEADS, B_NOPE + B_V)
    uk = jnp.pad(ukv[:, :, :B_NOPE], ((0, 0), (0, 0), (0, LANES - B_NOPE))).reshape(B_KV_LORA, B_HEADS * LANES)
    uv = ukv[:, :, B_NOPE:].reshape(B_KV_LORA, B_HEADS * B_V)
    return win, uq.astype(BF16), jnp.concatenate([uk, uv], axis=1).astype(BF16)


def _prep_c(w_qkv):
    nh = C_HEADS * C_HEAD_DIM
    return jnp.concatenate([w_qkv[:, :nh] * (C_HEAD_DIM ** -0.5), w_qkv[:, nh:]], axis=1).astype(BF16)


def _prep_ffn(w_in, conv_w, conv_b, w_out):
    d = w_in.shape[0]
    nc = FFN_DIM // FFN_CHUNK

    def chunked(a):
        r = a.shape[0]
        g = a[:, :FFN_DIM].reshape(r, nc, FFN_CHUNK)
        v = a[:, FFN_DIM:].reshape(r, nc, FFN_CHUNK)
        return jnp.concatenate([g, v], axis=2).transpose(1, 0, 2)

    win = chunked(w_in).astype(BF16)
    cw = chunked(jnp.concatenate([conv_w, conv_b[None, :]], axis=0)).astype(F32)
    return win, cw, w_out.reshape(nc, FFN_CHUNK, d).astype(BF16)


def _rope_tables_a(seq_len):
    half = A_HEAD_DIM // 2
    inv = ROPE_THETA ** (-jnp.arange(0, A_HEAD_DIM, 2, dtype=F32) / A_HEAD_DIM)
    ang = jnp.arange(seq_len).astype(F32)[:, None] * inv[None, :]
    cos, sin = jnp.cos(ang), jnp.sin(ang)
    reps = LANES // A_HEAD_DIM
    return jnp.tile(jnp.concatenate([cos, cos], axis=1), (1, reps)), jnp.tile(jnp.concatenate([-sin, sin], axis=1), (1, reps))


def _rope_tables_b(seq_len):
    half = B_ROPE // 2
    inv = ROPE_THETA ** (-jnp.arange(0, B_ROPE, 2, dtype=F32) / B_ROPE)
    ang = jnp.arange(seq_len).astype(F32)[:, None] * inv[None, :]
    cos, sin = jnp.cos(ang), jnp.sin(ang)
    zeros = lambda c: jnp.zeros((seq_len, c), F32)
    ones = lambda c: jnp.ones((seq_len, c), F32)
    tail = LANES - B_NOPE - B_ROPE
    cos_t = jnp.concatenate([ones(B_NOPE), cos, cos, ones(tail)], axis=1)
    sa = jnp.concatenate([zeros(B_NOPE), -sin, zeros(half), zeros(tail)], axis=1)
    sb = jnp.concatenate([zeros(B_NOPE), zeros(half), sin, zeros(tail)], axis=1)
    return cos_t, sa, sb


def _trunk(x, seq_len, norm_mix, norm_ffn, norm_final, layers, ffns, a_sink, b_norms, c_rpb):
    depth = len(layers)
    rope_a = _rope_tables_a(seq_len)
    rope_b = _rope_tables_b(seq_len)
    for i in range(depth):
        kind, j = i % 3, i // 3
        if kind == 0:
            w_qkv, w_o = layers[i]
            qkv = _norm_proj(x, norm_mix[i], w_qkv, seq_len, rope=rope_a, rope_cols=A_Q_COLS + A_KV_COLS)
            o = _attn_a(qkv, a_sink[j], seq_len)
        elif kind == 1:
            w_in, w_uq, w_ukv, w_o = layers[i]
            q, k, v = _proj_b(x, norm_mix[i], w_in, b_norms[0][j], b_norms[1][j], w_uq, w_ukv, rope_b, seq_len)
            o = _attn_b(q, k, v, seq_len)
        else:
            w_qkv, w_o = layers[i]
            qkv = _norm_proj(x, norm_mix[i], w_qkv, seq_len)
            o = _attn_c(qkv, c_rpb[j], seq_len)
        x = _oproj(x, o, w_o)
        fin = norm_final if i == depth - 1 else None
        x = _ffn(x, norm_ffn[i], *ffns[i], seq_len, final_gain=fin)
    return x


def kernel(x_prompt, x_sample, norm_mix, norm_ffn, norm_final, a_w_qkv, a_w_o, a_sink, b_w_in, b_q_norm, b_kv_norm, b_w_uq, b_w_ukv, b_w_o, c_w_qkv, c_rpb, c_w_o, f_w_in, f_conv_w, f_conv_b, f_w_out):
    depth = norm_mix.shape[0]
    layers, ffns = [], []
    for i in range(depth):
        kind, j = i % 3, i // 3
        if kind == 0:
            layers.append(_prep_a(a_w_qkv[j], a_w_o[j]))
        elif kind == 1:
            layers.append(_prep_b(b_w_in[j], b_w_uq[j], b_w_ukv[j]) + (b_w_o[j].astype(BF16),))
        else:
            layers.append((_prep_c(c_w_qkv[j]), c_w_o[j].astype(BF16)))
        ffns.append(_prep_ffn(f_w_in[i], f_conv_w[i], f_conv_b[i], f_w_out[i]))
    outs = []
    for x in (x_prompt, x_sample):
        b, t, d = x.shape
        y = _trunk(x.reshape(b * t, d), t, norm_mix, norm_ffn, norm_final, layers, ffns, a_sink,
                   (b_q_norm, b_kv_norm), c_rpb)
        outs.append(y.reshape(b, t, d))
    return tuple(outs)
```

```python
import functools

import numpy as np
import jax
import jax.numpy as jnp
from jax import lax
from jax.experimental import pallas as pl
from jax.experimental.pallas import tpu as pltpu

F32 = jnp.float32
BF16 = jnp.bfloat16

D_MODEL = 1024
NORM_EPS = 1e-6
ROPE_THETA = 10000.0
NEG_INF = -1e30
LOG2_E = 1.4426950408889634

LANES = 128
BF16_ROWS = 16
VMEM_LIMIT = 52 * 1024 * 1024

A_HEADS, A_KV_HEADS, A_HEAD_DIM, A_WINDOW = 16, 4, 64, 128
A_GROUP = A_HEADS // A_KV_HEADS
A_Q_COLS = A_HEADS * A_HEAD_DIM
A_KV_COLS = A_KV_HEADS * A_HEAD_DIM
B_HEADS, B_Q_LORA, B_KV_LORA, B_NOPE, B_ROPE, B_V = 16, 384, 256, 64, 32, 64
B_IN_PAD = B_Q_LORA + B_KV_LORA + LANES
GRID_W, C_HEADS, C_HEAD_DIM, C_WIN_ROWS, C_WIN_COLS = 64, 16, 64, 8, 16
C_QROWS = 2
C_KROWS = 10
FFN_DIM = 2816
FFN_CHUNK = 128


def _cparams(n_axes):
    return pltpu.CompilerParams(dimension_semantics=("arbitrary",) * n_axes, vmem_limit_bytes=VMEM_LIMIT)


def _resident(shape):
    zeros = (0,) * len(shape)
    return pl.BlockSpec(shape, lambda *_: zeros, pipeline_mode=pl.Buffered(1))


def _rms(x, g):
    return x * lax.rsqrt(jnp.mean(x * x, axis=-1, keepdims=True) + NORM_EPS) * g


def _low_half():
    return lax.broadcasted_iota(jnp.int32, (1, LANES), 1) < (LANES // 2)


def _split_pair(x, lo):
    zero = jnp.zeros_like(x)
    return jnp.concatenate([jnp.where(lo, x, zero), jnp.where(lo, zero, x)], axis=0)


def _dot_nt(a, b):
    return lax.dot_general(a, b, (((1,), (1,)), ((), ())), preferred_element_type=F32)


def _proj_kernel(*refs, n_out, q_cols, rope_cols, chunk):
    if rope_cols:
        x_ref, g_ref, w_ref, cos_ref, sin_ref, o_ref = refs
    else:
        x_ref, g_ref, w_ref, o_ref = refs
    h = _rms(x_ref[...], g_ref[...]).astype(BF16)
    if rope_cols:
        lane = lax.broadcasted_iota(jnp.int32, (1, LANES), 1)
        first = (lane % A_HEAD_DIM) < (A_HEAD_DIM // 2)
        cos = cos_ref[...]
        sin = sin_ref[...]
    for c0 in range(0, n_out, chunk):
        y = jnp.dot(h, w_ref[:, c0:c0 + chunk], preferred_element_type=F32)
        for s in range(0, chunk, LANES):
            yc = y[:, s:s + LANES]
            if c0 + s < rope_cols:
                rot = jnp.where(first, pltpu.roll(yc, LANES - A_HEAD_DIM // 2, 1),
                                pltpu.roll(yc, A_HEAD_DIM // 2, 1))
                yc = yc * cos + rot * sin
            if c0 + s < q_cols:
                yc = yc * LOG2_E
            o_ref[:, c0 + s:c0 + s + LANES] = yc.astype(BF16)


def _norm_proj(x, gain, w, seq_len, q_cols, rope=None, rope_cols=0, tm=512, chunk=512):
    n, d = x.shape
    n_out = w.shape[1]
    assert n % tm == 0 and seq_len % tm == 0 and n_out % chunk == 0
    tiles_per_seq = seq_len // tm
    in_specs = [pl.BlockSpec((tm, d), lambda i: (i, 0)), _resident((1, d)), _resident((d, n_out))]
    args = [x, gain.reshape(1, d), w]
    if rope_cols:
        in_specs += [pl.BlockSpec((tm, LANES), lambda i: (i % tiles_per_seq, 0))] * 2
        args += list(rope)
    return pl.pallas_call(
        functools.partial(_proj_kernel, n_out=n_out, q_cols=q_cols, rope_cols=rope_cols, chunk=chunk),
        out_shape=jax.ShapeDtypeStruct((n, n_out), BF16),
        grid=(n // tm,),
        in_specs=in_specs,
        out_specs=pl.BlockSpec((tm, n_out), lambda i: (i, 0)),
        compiler_params=_cparams(1),
    )(*args)


def _attn_a_kernel(sink_ref, q_ref, kp_ref, km_ref, kn_ref, vp_ref, vm_ref, vn_ref, band_ref, o_ref,
                   kbuf, vbuf, *, tq):
    w = A_WINDOW
    i = pl.program_id(1)
    n_i = pl.num_programs(1)
    kbuf[0:w] = kp_ref[...]
    kbuf[w:w + tq] = km_ref[...]
    kbuf[w + tq:] = kn_ref[...]
    vbuf[0:w] = vp_ref[...]
    vbuf[w:w + tq] = vm_ref[...]
    vbuf[w + tq:] = vn_ref[...]
    lo = _low_half()
    kidx = lax.broadcasted_iota(jnp.int32, (1, 3 * w), 1)
    band = band_ref[...]
    n_sub = tq // w
    for j in range(n_sub):
        bias = band
        if j == 0:
            bias = bias + jnp.where(kidx < w, jnp.where(i == 0, NEG_INF, 0.0), 0.0)
        if j == n_sub - 1:
            bias = bias + jnp.where(kidx >= 2 * w, jnp.where(i == n_i - 1, NEG_INF, 0.0), 0.0)
        for p in range(A_KV_HEADS // 2):
            kw = kbuf[j * w:(j + 3) * w, p * LANES:(p + 1) * LANES]
            vw = vbuf[j * w:(j + 3) * w, p * LANES:(p + 1) * LANES]
            qg = jnp.concatenate(
                [q_ref[j * w:(j + 1) * w, (p * A_GROUP + g) * LANES:(p * A_GROUP + g + 1) * LANES]
                 for g in range(A_GROUP)], axis=0)
            s = _dot_nt(_split_pair(qg, lo), kw)
            probs, invs = [], []
            for e in range(2):
                for g in range(A_GROUP):
                    r0 = (e * A_GROUP + g) * w
                    sh = s[r0:r0 + w] + bias
                    sk = sink_ref[(2 * p + e) * A_GROUP + g] * LOG2_E
                    m = jnp.maximum(jnp.max(sh, axis=-1, keepdims=True), sk)
                    pe = jnp.exp2(sh - m)
                    invs.append(1.0 / (jnp.sum(pe, axis=-1, keepdims=True) + jnp.exp2(sk - m)))
                    probs.append(pe.astype(BF16))
            r = jnp.dot(jnp.concatenate(probs, axis=0), vw, preferred_element_type=F32)
            for g in range(A_GROUP):
                o = jnp.where(lo, r[g * w:(g + 1) * w] * invs[g],
                              r[(A_GROUP + g) * w:(A_GROUP + g + 1) * w] * invs[A_GROUP + g])
                c = (p * A_GROUP + g) * LANES
                o_ref[j * w:(j + 1) * w, c:c + LANES] = o.astype(BF16)


def _attn_a(qkv, sink, seq_len, tq=512):
    n = qkv.shape[0]
    w = A_WINDOW
    tq = min(tq, seq_len)
    assert seq_len % tq == 0 and tq % w == 0 and n % seq_len == 0
    nq, nb, r = seq_len // tq, seq_len // w, tq // w
    kcol, vcol = A_Q_COLS // A_KV_COLS, A_Q_COLS // A_KV_COLS + 1
    rel = np.arange(3 * w)[None, :] - w - np.arange(w)[:, None]
    band = jnp.asarray(np.where(np.abs(rel) <= w, 0.0, NEG_INF), F32)

    def main(col):
        return pl.BlockSpec((tq, A_KV_COLS), lambda b, i: (b * nq + i, col))

    def prev(col):
        return pl.BlockSpec((w, A_KV_COLS), lambda b, i: (b * nb + jnp.maximum(i * r - 1, 0), col))

    def nxt(col):
        return pl.BlockSpec((w, A_KV_COLS), lambda b, i: (b * nb + jnp.minimum((i + 1) * r, nb - 1), col))

    return pl.pallas_call(
        functools.partial(_attn_a_kernel, tq=tq),
        out_shape=jax.ShapeDtypeStruct((n, A_Q_COLS), BF16),
        grid=(n // seq_len, nq),
        in_specs=[pl.BlockSpec(memory_space=pltpu.SMEM),
                  pl.BlockSpec((tq, A_Q_COLS), lambda b, i: (b * nq + i, 0)),
                  prev(kcol), main(kcol), nxt(kcol), prev(vcol), main(vcol), nxt(vcol),
                  _resident((w, 3 * w))],
        out_specs=pl.BlockSpec((tq, A_Q_COLS), lambda b, i: (b * nq + i, 0)),
        scratch_shapes=[pltpu.VMEM((tq + 2 * w, A_KV_COLS), BF16), pltpu.VMEM((tq + 2 * w, A_KV_COLS), BF16)],
        compiler_params=_cparams(2),
    )(sink, qkv, qkv, qkv, qkv, qkv, qkv, qkv, band)


B_TK = 512


def _proj_b_kernel(x_ref, g_ref, win_ref, qn_ref, kvn_ref, wuq_ref, wuk_ref, wvt_ref, cos_ref, sa_ref, sb_ref,
                   q_ref, k_ref, vt_ref):
    h = _rms(x_ref[...], g_ref[...]).astype(BF16)
    c = jnp.dot(h, win_ref[...], preferred_element_type=F32)
    cq = _rms(c[:, :B_Q_LORA], qn_ref[...]).astype(BF16)
    ckv = _rms(c[:, B_Q_LORA:B_Q_LORA + B_KV_LORA], kvn_ref[...]).astype(BF16)
    cos, sa, sb = cos_ref[...], sa_ref[...], sb_ref[...]
    half = B_ROPE // 2

    def rope(y):
        return y * cos + pltpu.roll(y, LANES - half, 1) * sa + pltpu.roll(y, half, 1) * sb

    kr = rope(c[:, B_Q_LORA + B_KV_LORA:])
    scale = (B_NOPE + B_ROPE) ** -0.5 * LOG2_E
    for hh in range(B_HEADS):
        sl = slice(hh * LANES, (hh + 1) * LANES)
        q = jnp.dot(cq, wuq_ref[:, sl], preferred_element_type=F32)
        q_ref[:, sl] = (rope(q) * scale).astype(BF16)
        k = jnp.dot(ckv, wuk_ref[:, sl], preferred_element_type=F32)
        k_ref[:, sl] = (k + kr).astype(BF16)
    vt_ref[0] = _dot_nt(wvt_ref[...], ckv).astype(BF16)


def _proj_b(x, gain, w_in, q_norm, kv_norm, w_uq, w_uk, w_vt, tables, seq_len):
    n, d = x.shape
    tm = B_TK
    assert n % tm == 0 and seq_len % tm == 0
    tps = seq_len // tm
    nq = B_HEADS * LANES
    nv = B_HEADS * B_V
    tab = pl.BlockSpec((tm, LANES), lambda i: (i % tps, 0))
    row = lambda c: pl.BlockSpec((tm, c), lambda i: (i, 0))
    return pl.pallas_call(
        _proj_b_kernel,
        out_shape=(jax.ShapeDtypeStruct((n, nq), BF16), jax.ShapeDtypeStruct((n, nq), BF16),
                   jax.ShapeDtypeStruct((n // tm, nv, tm), BF16)),
        grid=(n // tm,),
        in_specs=[row(d), _resident((1, d)), _resident((d, B_IN_PAD)), _resident((1, B_Q_LORA)),
                  _resident((1, B_KV_LORA)), _resident((B_Q_LORA, nq)), _resident((B_KV_LORA, nq)),
                  _resident((nv, B_KV_LORA)), tab, tab, tab],
        out_specs=(row(nq), row(nq), pl.BlockSpec((1, nv, tm), lambda i: (i, 0, 0))),
        compiler_params=_cparams(1),
    )(x, gain.reshape(1, d), w_in, q_norm.reshape(1, -1), kv_norm.reshape(1, -1), w_uq, w_uk, w_vt, *tables)


def _attn_b_kernel(q_ref, k_ref, vt_ref, o_ref, *, tq, n_k, unroll):
    row_lo = lax.broadcasted_iota(jnp.int32, (LANES, 1), 0) < (LANES // 2)
    q0, q1 = q_ref[:, :LANES], q_ref[:, LANES:]

    def update(kc, q, m_old, l_old):
        s = _dot_nt(kc, q)
        m_new = jnp.maximum(m_old, jnp.max(s, axis=0, keepdims=True))
        alpha = jnp.exp2(m_old - m_new)
        pe = jnp.exp2(s - m_new)
        return m_new, alpha * l_old + jnp.sum(pe, axis=0, keepdims=True), alpha, pe.astype(BF16)

    def body(j, carry):
        m0, m1, l0, l1, acc = carry
        r0 = pl.multiple_of(j * B_TK, B_TK)
        kc = k_ref[pl.ds(r0, B_TK), :]
        m0, l0, a0, p0 = update(kc[:, :LANES], q0, m0, l0)
        m1, l1, a1, p1 = update(kc[:, LANES:], q1, m1, l1)
        vt = vt_ref[j]
        zero = jnp.zeros_like(vt)
        acc = (acc * jnp.where(row_lo, a0, a1)
               + jnp.dot(jnp.where(row_lo, vt, zero), p0, preferred_element_type=F32)
               + jnp.dot(jnp.where(row_lo, zero, vt), p1, preferred_element_type=F32))
        return m0, m1, l0, l1, acc

    vec = lambda v: jnp.full((1, tq), v, F32)
    init = (vec(NEG_INF), vec(NEG_INF), vec(0.0), vec(0.0), jnp.zeros((LANES, tq), F32))
    _, _, l0, l1, acc = lax.fori_loop(0, n_k, body, init, unroll=unroll)
    o_ref[...] = (acc * jnp.where(row_lo, 1.0 / l0, 1.0 / l1)).T.astype(BF16)


def _attn_b(q, k, vt, seq_len, tq=512, unroll=4):
    n = q.shape[0]
    tq = min(tq, seq_len)
    n_k = seq_len // B_TK
    unroll = min(unroll, n_k)
    assert seq_len % tq == 0 and seq_len % B_TK == 0 and n % seq_len == 0 and n_k % unroll == 0
    nq = seq_len // tq
    nv = vt.shape[1]
    return pl.pallas_call(
        functools.partial(_attn_b_kernel, tq=tq, n_k=n_k, unroll=unroll),
        out_shape=jax.ShapeDtypeStruct((n, nv), BF16),
        grid=(n // seq_len, B_HEADS // 2, nq),
        in_specs=[pl.BlockSpec((tq, 2 * LANES), lambda b, p, i: (b * nq + i, p)),
                  pl.BlockSpec((seq_len, 2 * LANES), lambda b, p, i: (b, p)),
                  pl.BlockSpec((n_k, LANES, B_TK), lambda b, p, i: (b, p, 0))],
        out_specs=pl.BlockSpec((tq, LANES), lambda b, p, i: (b * nq + i, p)),
        compiler_params=_cparams(3),
    )(q, k, vt)


def _attn_c_kernel(q_ref, k0, k1, k2, k3, k4, v0, v1, v2, v3, v4, tab_ref, o_ref):
    lo = _low_half()
    nq = q_ref.shape[0]
    for p in range(C_HEADS // 2):
        ps = slice(p * LANES, (p + 1) * LANES)
        kc = jnp.concatenate([r[:, ps] for r in (k0, k1, k2, k3, k4)], axis=0)
        vc = jnp.concatenate([r[:, ps] for r in (v0, v1, v2, v3, v4)], axis=0)
        s = _dot_nt(_split_pair(q_ref[:, ps], lo), kc)
        probs, invs = [], []
        for e in range(2):
            sh = s[e * nq:(e + 1) * nq] + tab_ref[0, 2 * p + e]
            m = jnp.max(sh, axis=-1, keepdims=True)
            pe = jnp.exp2(sh - m)
            invs.append(1.0 / jnp.sum(pe, axis=-1, keepdims=True))
            probs.append(pe.astype(BF16))
        r = jnp.dot(jnp.concatenate(probs, axis=0), vc, preferred_element_type=F32)
        o_ref[:, ps] = jnp.where(lo, r[:nq] * invs[0], r[nq:] * invs[1]).astype(BF16)


def _c_window_start(rb, rows):
    return jnp.clip(rb - 2, 0, rows // 2 - C_KROWS // 2)


def _c_bias_tables(rpb, rows):
    nrb = rows // C_QROWS
    col = np.arange(GRID_W)
    cs = np.clip(col - C_WIN_COLS // 2, 0, GRID_W - C_WIN_COLS)
    col_ok = (col[None, :] >= cs[:, None]) & (col[None, :] < cs[:, None] + C_WIN_COLS)
    pad = GRID_W - C_WIN_COLS
    padded = jnp.pad(rpb, ((0, 0), (0, 0), (pad, pad)))
    e = jnp.stack([padded[:, :, GRID_W - 1 - qc:2 * GRID_W - 1 - qc] for qc in range(GRID_W)], axis=2) * LOG2_E
    e = jnp.where(jnp.asarray(col_ok)[None, None], e, NEG_INF)
    masked = jnp.full((C_HEADS, GRID_W, GRID_W), NEG_INF, F32)
    tabs = []
    for rb in (0, 1, 2, nrb - 2, nrb - 1):
        w0 = 2 * int(np.clip(rb - 2, 0, rows // 2 - C_KROWS // 2))
        per_qrow = []
        for qrow in range(rb * C_QROWS, (rb + 1) * C_QROWS):
            rs = int(np.clip(qrow - C_WIN_ROWS // 2, 0, rows - C_WIN_ROWS))
            planes = [e[:, krow - qrow + C_WIN_ROWS - 1] if rs <= krow < rs + C_WIN_ROWS else masked
                      for krow in range(w0, w0 + C_KROWS)]
            per_qrow.append(jnp.stack(planes, axis=2))
        tabs.append(jnp.stack(per_qrow, axis=1).reshape(C_HEADS, C_QROWS * GRID_W, C_KROWS * GRID_W))
    return jnp.stack(tabs)


def _attn_c(qkv, rpb, seq_len):
    n = qkv.shape[0]
    rows = seq_len // GRID_W
    nrb = rows // C_QROWS
    assert seq_len % GRID_W == 0 and rows % 2 == 0 and rows >= C_KROWS + 2 and nrb >= 6
    tq = C_QROWS * GRID_W
    nkb = C_KROWS * GRID_W // tq
    nh = C_HEADS * C_HEAD_DIM
    tables = _c_bias_tables(rpb.astype(F32), rows)
    blocks_per_seq = seq_len // tq

    def kv(col, j):
        return pl.BlockSpec((tq, nh), lambda rb, b: (b * blocks_per_seq + _c_window_start(rb, rows) + j, col))

    def pattern(rb, b):
        pat = jnp.where(rb < 2, rb, jnp.where(rb >= nrb - 2, rb - (nrb - 2) + 3, 2))
        return (pat, 0, 0, 0)

    return pl.pallas_call(
        _attn_c_kernel,
        out_shape=jax.ShapeDtypeStruct((n, nh), BF16),
        grid=(nrb, n // seq_len),
        in_specs=[pl.BlockSpec((tq, nh), lambda rb, b: (b * blocks_per_seq + rb, 0))]
                 + [kv(1, j) for j in range(nkb)] + [kv(2, j) for j in range(nkb)]
                 + [pl.BlockSpec((1, C_HEADS, tq, C_KROWS * GRID_W), pattern)],
        out_specs=pl.BlockSpec((tq, nh), lambda rb, b: (b * blocks_per_seq + rb, 0)),
        compiler_params=_cparams(2),
    )(*([qkv] * (1 + 2 * nkb)), tables)


def _ffn_kernel(*refs, tm, n_chunks, fc, tiles_per_seq, final):
    if final:
        (xm_ref, xp_ref, xn_ref, om_ref, op_ref, on_ref, wo_ref, g_ref, win_ref, cw_ref, wout_ref, fg_ref,
         y_ref, o_scr, h_scr, a_scr) = refs
    else:
        (xm_ref, xp_ref, xn_ref, om_ref, op_ref, on_ref, wo_ref, g_ref, win_ref, cw_ref, wout_ref,
         y_ref, o_scr, h_scr, a_scr) = refs
    t = pl.program_id(0) % tiles_per_seq
    g = g_ref[...]
    nxt = lax.broadcasted_iota(jnp.int32, (BF16_ROWS, 1), 0) < BF16_ROWS // 2
    o_scr[0:tm] = om_ref[...]
    o_scr[tm:] = jnp.where(nxt, on_ref[...].astype(F32), op_ref[...].astype(F32)).astype(BF16)
    mixed = jnp.dot(o_scr[...], wo_ref[...], preferred_element_type=F32)
    x1 = xm_ref[...] + mixed[:tm]
    x1_halo = jnp.where(nxt, xn_ref[...], xp_ref[...]) + mixed[tm:]
    y_ref[...] = x1
    h_scr[0:tm] = _rms(x1, g).astype(BF16)
    inside = jnp.where(nxt, jnp.where(t == tiles_per_seq - 1, 0.0, 1.0), jnp.where(t == 0, 0.0, 1.0))
    h_scr[tm:] = jnp.where(inside > 0.0, _rms(x1_halo, g), 0.0).astype(BF16)
    hext = h_scr[...]
    m = tm + BF16_ROWS
    for c in range(n_chunks):
        u = jnp.dot(hext, win_ref[c], preferred_element_type=F32)
        cw = cw_ref[c]
        z = (cw[3:4] + pltpu.roll(u, 1, 0)[:tm] * cw[0:1] + u[:tm] * cw[1:2]
             + pltpu.roll(u, m - 1, 0)[:tm] * cw[2:3])
        gate, val = z[:, :fc], z[:, fc:]
        a_scr[:, c * fc:(c + 1) * fc] = (gate * (1.0 / (1.0 + jnp.exp(-gate))) * val).astype(BF16)
    y = y_ref[...] + jnp.dot(a_scr[...], wout_ref[...], preferred_element_type=F32)
    if final:
        y = _rms(y, fg_ref[...])
    y_ref[...] = y


def _mix_ffn(x, o, w_o, gain, w_in, cw, w_out, seq_len, final_gain=None, tm=512):
    n, d = x.shape
    n_chunks, _, fc2 = w_in.shape
    assert n % tm == 0 and seq_len % tm == 0 and tm % BF16_ROWS == 0 and o.shape == (n, w_o.shape[0])
    tps = seq_len // tm
    r = tm // BF16_ROWS
    nblk = n // BF16_ROWS
    final = final_gain is not None

    def tile_specs(width):
        return [pl.BlockSpec((tm, width), lambda i: (i, 0)),
                pl.BlockSpec((BF16_ROWS, width), lambda i: (jnp.maximum(i * r - 1, 0), 0)),
                pl.BlockSpec((BF16_ROWS, width), lambda i: (jnp.minimum((i + 1) * r, nblk - 1), 0))]

    in_specs = tile_specs(d) + tile_specs(o.shape[1]) + [
        _resident(w_o.shape), _resident((1, d)), _resident(w_in.shape), _resident(cw.shape), _resident(w_out.shape)]
    args = [x, x, x, o, o, o, w_o, gain.reshape(1, d), w_in, cw, w_out]
    if final:
        in_specs.append(_resident((1, d)))
        args.append(final_gain.reshape(1, d))
    return pl.pallas_call(
        functools.partial(_ffn_kernel, tm=tm, n_chunks=n_chunks, fc=fc2 // 2, tiles_per_seq=tps, final=final),
        out_shape=jax.ShapeDtypeStruct((n, d), F32),
        grid=(n // tm,),
        in_specs=in_specs,
        out_specs=pl.BlockSpec((tm, d), lambda i: (i, 0)),
        scratch_shapes=[pltpu.VMEM((tm + BF16_ROWS, o.shape[1]), BF16), pltpu.VMEM((tm + BF16_ROWS, d), BF16),
                        pltpu.VMEM((tm, w_out.shape[0]), BF16)],
        compiler_params=_cparams(1),
    )(*args)


def _a_head_order():
    order = []
    for p in range(A_KV_HEADS // 2):
        for g in range(A_GROUP):
            order += [(2 * p) * A_GROUP + g, (2 * p + 1) * A_GROUP + g]
    return np.asarray(order)


def _prep_a(w_qkv, w_o):
    cols = (_a_head_order()[:, None] * A_HEAD_DIM + np.arange(A_HEAD_DIM)[None, :]).reshape(-1)
    wq = w_qkv[:, :A_Q_COLS][:, cols] * (A_HEAD_DIM ** -0.5)
    w = jnp.concatenate([wq, w_qkv[:, A_Q_COLS:]], axis=1).astype(BF16)
    return w, w_o[cols, :].astype(BF16)


def _prep_b(w_in, w_uq, w_ukv):
    d = w_in.shape[0]
    nl = B_Q_LORA + B_KV_LORA
    win = jnp.concatenate([w_in[:, :nl], jnp.zeros((d, B_NOPE), F32), w_in[:, nl:],
                           jnp.zeros((d, LANES - B_NOPE - B_ROPE), F32)], axis=1).astype(BF16)
    uq = w_uq.reshape(B_Q_LORA, B_HEADS, B_NOPE + B_ROPE)
    uq = jnp.pad(uq, ((0, 0), (0, 0), (0, LANES - B_NOPE - B_ROPE))).reshape(B_Q_LORA, B_HEADS * LANES)
    ukv = w_ukv.reshape(B_KV_LORA, B_HEADS, B_NOPE + B_V)
    uk = jnp.pad(ukv[:, :, :B_NOPE], ((0, 0), (0, 0), (0, LANES - B_NOPE))).reshape(B_KV_LORA, B_HEADS * LANES)
    uv = ukv[:, :, B_NOPE:].reshape(B_KV_LORA, B_HEADS * B_V)
    return win, uq.astype(BF16), uk.astype(BF16), uv.T.astype(BF16)


def _prep_c(w_qkv):
    nh = C_HEADS * C_HEAD_DIM
    return jnp.concatenate([w_qkv[:, :nh] * (C_HEAD_DIM ** -0.5), w_qkv[:, nh:]], axis=1).astype(BF16)


def _prep_ffn(w_in, conv_w, conv_b, w_out):
    d = w_in.shape[0]
    nc = FFN_DIM // FFN_CHUNK

    def chunked(a):
        r = a.shape[0]
        g = a[:, :FFN_DIM].reshape(r, nc, FFN_CHUNK)
        v = a[:, FFN_DIM:].reshape(r, nc, FFN_CHUNK)
        return jnp.concatenate([g, v], axis=2).transpose(1, 0, 2)

    win = chunked(w_in).astype(BF16)
    cw = chunked(jnp.concatenate([conv_w, conv_b[None, :]], axis=0)).astype(F32)
    return win, cw, w_out.astype(BF16)


def _rope_tables_a(seq_len):
    half = A_HEAD_DIM // 2
    inv = ROPE_THETA ** (-jnp.arange(0, A_HEAD_DIM, 2, dtype=F32) / A_HEAD_DIM)
    ang = jnp.arange(seq_len).astype(F32)[:, None] * inv[None, :]
    cos, sin = jnp.cos(ang), jnp.sin(ang)
    reps = LANES // A_HEAD_DIM
    return jnp.tile(jnp.concatenate([cos, cos], axis=1), (1, reps)), jnp.tile(jnp.concatenate([-sin, sin], axis=1), (1, reps))


def _rope_tables_b(seq_len):
    half = B_ROPE // 2
    inv = ROPE_THETA ** (-jnp.arange(0, B_ROPE, 2, dtype=F32) / B_ROPE)
    ang = jnp.arange(seq_len).astype(F32)[:, None] * inv[None, :]
    cos, sin = jnp.cos(ang), jnp.sin(ang)
    zeros = lambda c: jnp.zeros((seq_len, c), F32)
    ones = lambda c: jnp.ones((seq_len, c), F32)
    tail = LANES - B_NOPE - B_ROPE
    cos_t = jnp.concatenate([ones(B_NOPE), cos, cos, ones(tail)], axis=1)
    sa = jnp.concatenate([zeros(B_NOPE), -sin, zeros(half), zeros(tail)], axis=1)
    sb = jnp.concatenate([zeros(B_NOPE), zeros(half), sin, zeros(tail)], axis=1)
    return cos_t, sa, sb


def _trunk(x, seq_len, norm_mix, norm_ffn, norm_final, layers, ffns, a_sink, b_norms, c_rpb):
    depth = len(layers)
    rope_a = _rope_tables_a(seq_len)
    rope_b = _rope_tables_b(seq_len)
    for i in range(depth):
        kind, j = i % 3, i // 3
        if kind == 0:
            w_qkv, w_o = layers[i]
            qkv = _norm_proj(x, norm_mix[i], w_qkv, seq_len, A_Q_COLS, rope=rope_a, rope_cols=A_Q_COLS + A_KV_COLS)
            o = _attn_a(qkv, a_sink[j], seq_len)
        elif kind == 1:
            w_in, w_uq, w_uk, w_vt, w_o = layers[i]
            q, k, vt = _proj_b(x, norm_mix[i], w_in, b_norms[0][j], b_norms[1][j], w_uq, w_uk, w_vt, rope_b, seq_len)
            o = _attn_b(q, k, vt, seq_len)
        else:
            w_qkv, w_o = layers[i]
            qkv = _norm_proj(x, norm_mix[i], w_qkv, seq_len, C_HEADS * C_HEAD_DIM)
            o = _attn_c(qkv, c_rpb[j], seq_len)
        fin = norm_final if i == depth - 1 else None
        x = _mix_ffn(x, o, w_o, norm_ffn[i], *ffns[i], seq_len, final_gain=fin)
    return x


def kernel(x_prompt, x_sample, norm_mix, norm_ffn, norm_final, a_w_qkv, a_w_o, a_sink, b_w_in, b_q_norm, b_kv_norm, b_w_uq, b_w_ukv, b_w_o, c_w_qkv, c_rpb, c_w_o, f_w_in, f_conv_w, f_conv_b, f_w_out):
    depth = norm_mix.shape[0]
    layers, ffns = [], []
    for i in range(depth):
        kind, j = i % 3, i // 3
        if kind == 0:
            layers.append(_prep_a(a_w_qkv[j], a_w_o[j]))
        elif kind == 1:
            layers.append(_prep_b(b_w_in[j], b_w_uq[j], b_w_ukv[j]) + (b_w_o[j].astype(BF16),))
        else:
            layers.append((_prep_c(c_w_qkv[j]), c_w_o[j].astype(BF16)))
        ffns.append(_prep_ffn(f_w_in[i], f_conv_w[i], f_conv_b[i], f_w_out[i]))
    outs = []
    for x in (x_prompt, x_sample):
        b, t, d = x.shape
        y = _trunk(x.reshape(b * t, d), t, norm_mix, norm_ffn, norm_final, layers, ffns, a_sink,
                   (b_q_norm, b_kv_norm), c_rpb)
        outs.append(y.reshape(b, t, d))
    return tuple(outs)
```

```python
import functools

import numpy as np
import jax
import jax.numpy as jnp
from jax import lax
from jax.experimental import pallas as pl
from jax.experimental.pallas import tpu as pltpu

F32 = jnp.float32
BF16 = jnp.bfloat16

D_MODEL = 1024
NORM_EPS = 1e-6
ROPE_THETA = 10000.0
NEG_INF = -1e30
LOG2_E = 1.4426950408889634

LANES = 128
BF16_ROWS = 16
VMEM_LIMIT = 52 * 1024 * 1024

A_HEADS, A_KV_HEADS, A_HEAD_DIM, A_WINDOW = 16, 4, 64, 128
A_GROUP = A_HEADS // A_KV_HEADS
A_Q_COLS = A_HEADS * A_HEAD_DIM
A_KV_COLS = A_KV_HEADS * A_HEAD_DIM
B_HEADS, B_Q_LORA, B_KV_LORA, B_NOPE, B_ROPE, B_V = 16, 384, 256, 64, 32, 64
B_IN_PAD = B_Q_LORA + B_KV_LORA + LANES
GRID_W, C_HEADS, C_HEAD_DIM, C_WIN_ROWS, C_WIN_COLS = 64, 16, 64, 8, 16
C_QROWS = 2
C_KROWS = 10
FFN_DIM = 2816
FFN_CHUNK = 256


def _cparams(n_axes):
    return pltpu.CompilerParams(dimension_semantics=("arbitrary",) * n_axes, vmem_limit_bytes=VMEM_LIMIT)


def _resident(shape):
    zeros = (0,) * len(shape)
    return pl.BlockSpec(shape, lambda *_: zeros, pipeline_mode=pl.Buffered(1))


def _rms(x, g):
    return x * lax.rsqrt(jnp.mean(x * x, axis=-1, keepdims=True) + NORM_EPS) * g


def _low_half():
    return lax.broadcasted_iota(jnp.int32, (1, LANES), 1) < (LANES // 2)


def _split_pair(x, lo):
    zero = jnp.zeros_like(x)
    return jnp.concatenate([jnp.where(lo, x, zero), jnp.where(lo, zero, x)], axis=0)


def _dot_nt(a, b):
    return lax.dot_general(a, b, (((1,), (1,)), ((), ())), preferred_element_type=F32)


def _proj_kernel(*refs, n_out, q_cols, rope_cols, chunk):
    if rope_cols:
        x_ref, g_ref, w_ref, cos_ref, sin_ref, o_ref = refs
    else:
        x_ref, g_ref, w_ref, o_ref = refs
    h = _rms(x_ref[...], g_ref[...]).astype(BF16)
    if rope_cols:
        lane = lax.broadcasted_iota(jnp.int32, (1, LANES), 1)
        first = (lane % A_HEAD_DIM) < (A_HEAD_DIM // 2)
        cos = cos_ref[...]
        sin = sin_ref[...]
    for c0 in range(0, n_out, chunk):
        y = jnp.dot(h, w_ref[:, c0:c0 + chunk], preferred_element_type=F32)
        for s in range(0, chunk, LANES):
            yc = y[:, s:s + LANES]
            if c0 + s < rope_cols:
                rot = jnp.where(first, pltpu.roll(yc, LANES - A_HEAD_DIM // 2, 1),
                                pltpu.roll(yc, A_HEAD_DIM // 2, 1))
                yc = yc * cos + rot * sin
            if c0 + s < q_cols:
                yc = yc * LOG2_E
            o_ref[:, c0 + s:c0 + s + LANES] = yc.astype(BF16)


def _norm_proj(x, gain, w, seq_len, q_cols, rope=None, rope_cols=0, tm=512, chunk=512):
    n, d = x.shape
    n_out = w.shape[1]
    assert n % tm == 0 and seq_len % tm == 0 and n_out % chunk == 0
    tiles_per_seq = seq_len // tm
    in_specs = [pl.BlockSpec((tm, d), lambda i: (i, 0)), _resident((1, d)), _resident((d, n_out))]
    args = [x, gain.reshape(1, d), w]
    if rope_cols:
        in_specs += [pl.BlockSpec((tm, LANES), lambda i: (i % tiles_per_seq, 0))] * 2
        args += list(rope)
    return pl.pallas_call(
        functools.partial(_proj_kernel, n_out=n_out, q_cols=q_cols, rope_cols=rope_cols, chunk=chunk),
        out_shape=jax.ShapeDtypeStruct((n, n_out), BF16),
        grid=(n // tm,),
        in_specs=in_specs,
        out_specs=pl.BlockSpec((tm, n_out), lambda i: (i, 0)),
        compiler_params=_cparams(1),
    )(*args)


def _attn_a_kernel(sink_ref, q_ref, kp_ref, km_ref, kn_ref, vp_ref, vm_ref, vn_ref, band_ref, o_ref,
                   kbuf, vbuf, *, tq):
    w = A_WINDOW
    i = pl.program_id(1)
    n_i = pl.num_programs(1)
    kbuf[0:w] = kp_ref[...]
    kbuf[w:w + tq] = km_ref[...]
    kbuf[w + tq:] = kn_ref[...]
    vbuf[0:w] = vp_ref[...]
    vbuf[w:w + tq] = vm_ref[...]
    vbuf[w + tq:] = vn_ref[...]
    lo = _low_half()
    kidx = lax.broadcasted_iota(jnp.int32, (1, 3 * w), 1)
    band = band_ref[...]
    n_sub = tq // w
    for j in range(n_sub):
        bias = band
        if j == 0:
            bias = bias + jnp.where(kidx < w, jnp.where(i == 0, NEG_INF, 0.0), 0.0)
        if j == n_sub - 1:
            bias = bias + jnp.where(kidx >= 2 * w, jnp.where(i == n_i - 1, NEG_INF, 0.0), 0.0)
        for p in range(A_KV_HEADS // 2):
            kw = kbuf[j * w:(j + 3) * w, p * LANES:(p + 1) * LANES]
            vw = vbuf[j * w:(j + 3) * w, p * LANES:(p + 1) * LANES]
            qg = jnp.concatenate(
                [q_ref[j * w:(j + 1) * w, (p * A_GROUP + g) * LANES:(p * A_GROUP + g + 1) * LANES]
                 for g in range(A_GROUP)], axis=0)
            s = _dot_nt(_split_pair(qg, lo), kw)
            probs, invs = [], []
            for e in range(2):
                for g in range(A_GROUP):
                    r0 = (e * A_GROUP + g) * w
                    sh = s[r0:r0 + w] + bias
                    sk = sink_ref[(2 * p + e) * A_GROUP + g] * LOG2_E
                    m = jnp.maximum(jnp.max(sh, axis=-1, keepdims=True), sk)
                    pe = jnp.exp2(sh - m)
                    invs.append(1.0 / (jnp.sum(pe, axis=-1, keepdims=True) + jnp.exp2(sk - m)))
                    probs.append(pe.astype(BF16))
            r = jnp.dot(jnp.concatenate(probs, axis=0), vw, preferred_element_type=F32)
            for g in range(A_GROUP):
                o = jnp.where(lo, r[g * w:(g + 1) * w] * invs[g],
                              r[(A_GROUP + g) * w:(A_GROUP + g + 1) * w] * invs[A_GROUP + g])
                c = (p * A_GROUP + g) * LANES
                o_ref[j * w:(j + 1) * w, c:c + LANES] = o.astype(BF16)


def _attn_a(qkv, sink, seq_len, tq=1024):
    n = qkv.shape[0]
    w = A_WINDOW
    tq = min(tq, seq_len)
    assert seq_len % tq == 0 and tq % w == 0 and n % seq_len == 0
    nq, nb, r = seq_len // tq, seq_len // w, tq // w
    kcol, vcol = A_Q_COLS // A_KV_COLS, A_Q_COLS // A_KV_COLS + 1
    rel = np.arange(3 * w)[None, :] - w - np.arange(w)[:, None]
    band = jnp.asarray(np.where(np.abs(rel) <= w, 0.0, NEG_INF), F32)

    def main(col):
        return pl.BlockSpec((tq, A_KV_COLS), lambda b, i: (b * nq + i, col))

    def prev(col):
        return pl.BlockSpec((w, A_KV_COLS), lambda b, i: (b * nb + jnp.maximum(i * r - 1, 0), col))

    def nxt(col):
        return pl.BlockSpec((w, A_KV_COLS), lambda b, i: (b * nb + jnp.minimum((i + 1) * r, nb - 1), col))

    return pl.pallas_call(
        functools.partial(_attn_a_kernel, tq=tq),
        out_shape=jax.ShapeDtypeStruct((n, A_Q_COLS), BF16),
        grid=(n // seq_len, nq),
        in_specs=[pl.BlockSpec(memory_space=pltpu.SMEM),
                  pl.BlockSpec((tq, A_Q_COLS), lambda b, i: (b * nq + i, 0)),
                  prev(kcol), main(kcol), nxt(kcol), prev(vcol), main(vcol), nxt(vcol),
                  _resident((w, 3 * w))],
        out_specs=pl.BlockSpec((tq, A_Q_COLS), lambda b, i: (b * nq + i, 0)),
        scratch_shapes=[pltpu.VMEM((tq + 2 * w, A_KV_COLS), BF16), pltpu.VMEM((tq + 2 * w, A_KV_COLS), BF16)],
        compiler_params=_cparams(2),
    )(sink, qkv, qkv, qkv, qkv, qkv, qkv, qkv, band)


B_TK = 1024


def _proj_b_kernel(x_ref, g_ref, win_ref, qn_ref, kvn_ref, wuq_ref, wuk_ref, wvt_ref, cos_ref, sa_ref, sb_ref,
                   q_ref, k_ref, vt_ref):
    h = _rms(x_ref[...], g_ref[...]).astype(BF16)
    c = jnp.dot(h, win_ref[...], preferred_element_type=F32)
    cq = _rms(c[:, :B_Q_LORA], qn_ref[...]).astype(BF16)
    ckv = _rms(c[:, B_Q_LORA:B_Q_LORA + B_KV_LORA], kvn_ref[...]).astype(BF16)
    cos, sa, sb = cos_ref[...], sa_ref[...], sb_ref[...]
    half = B_ROPE // 2

    def rope(y):
        return y * cos + pltpu.roll(y, LANES - half, 1) * sa + pltpu.roll(y, half, 1) * sb

    kr = rope(c[:, B_Q_LORA + B_KV_LORA:])
    scale = (B_NOPE + B_ROPE) ** -0.5 * LOG2_E
    for hh in range(B_HEADS):
        sl = slice(hh * LANES, (hh + 1) * LANES)
        q = jnp.dot(cq, wuq_ref[:, sl], preferred_element_type=F32)
        q_ref[:, sl] = (rope(q) * scale).astype(BF16)
        k = jnp.dot(ckv, wuk_ref[:, sl], preferred_element_type=F32)
        k_ref[:, sl] = (k + kr).astype(BF16)
    vt_ref[0] = _dot_nt(wvt_ref[...], ckv).astype(BF16)


def _proj_b(x, gain, w_in, q_norm, kv_norm, w_uq, w_uk, w_vt, tables, seq_len):
    n, d = x.shape
    tm = B_TK
    assert n % tm == 0 and seq_len % tm == 0
    tps = seq_len // tm
    nq = B_HEADS * LANES
    nv = B_HEADS * B_V
    tab = pl.BlockSpec((tm, LANES), lambda i: (i % tps, 0))
    row = lambda c: pl.BlockSpec((tm, c), lambda i: (i, 0))
    return pl.pallas_call(
        _proj_b_kernel,
        out_shape=(jax.ShapeDtypeStruct((n, nq), BF16), jax.ShapeDtypeStruct((n, nq), BF16),
                   jax.ShapeDtypeStruct((n // tm, nv, tm), BF16)),
        grid=(n // tm,),
        in_specs=[row(d), _resident((1, d)), _resident((d, B_IN_PAD)), _resident((1, B_Q_LORA)),
                  _resident((1, B_KV_LORA)), _resident((B_Q_LORA, nq)), _resident((B_KV_LORA, nq)),
                  _resident((nv, B_KV_LORA)), tab, tab, tab],
        out_specs=(row(nq), row(nq), pl.BlockSpec((1, nv, tm), lambda i: (i, 0, 0))),
        compiler_params=_cparams(1),
    )(x, gain.reshape(1, d), w_in, q_norm.reshape(1, -1), kv_norm.reshape(1, -1), w_uq, w_uk, w_vt, *tables)


def _attn_b_kernel(q_ref, k_ref, vt_ref, o_ref, *, tq, n_k, unroll):
    row_lo = lax.broadcasted_iota(jnp.int32, (LANES, 1), 0) < (LANES // 2)
    q0, q1 = q_ref[:, :LANES], q_ref[:, LANES:]

    def update(kc, q, m_old, l_old):
        s = _dot_nt(kc, q)
        m_new = jnp.maximum(m_old, jnp.max(s, axis=0, keepdims=True))
        alpha = jnp.exp2(m_old - m_new)
        pe = jnp.exp2(s - m_new)
        return m_new, alpha * l_old + jnp.sum(pe, axis=0, keepdims=True), alpha, pe.astype(BF16)

    def body(j, carry):
        m0, m1, l0, l1, acc = carry
        r0 = pl.multiple_of(j * B_TK, B_TK)
        kc = k_ref[pl.ds(r0, B_TK), :]
        m0, l0, a0, p0 = update(kc[:, :LANES], q0, m0, l0)
        m1, l1, a1, p1 = update(kc[:, LANES:], q1, m1, l1)
        vt = vt_ref[j]
        zero = jnp.zeros_like(vt)
        acc = (acc * jnp.where(row_lo, a0, a1)
               + jnp.dot(jnp.where(row_lo, vt, zero), p0, preferred_element_type=F32)
               + jnp.dot(jnp.where(row_lo, zero, vt), p1, preferred_element_type=F32))
        return m0, m1, l0, l1, acc

    vec = lambda v: jnp.full((1, tq), v, F32)
    init = (vec(NEG_INF), vec(NEG_INF), vec(0.0), vec(0.0), jnp.zeros((LANES, tq), F32))
    _, _, l0, l1, acc = lax.fori_loop(0, n_k, body, init, unroll=unroll)
    o_ref[...] = (acc * jnp.where(row_lo, 1.0 / l0, 1.0 / l1)).T.astype(BF16)


def _attn_b(q, k, vt, seq_len, tq=1024, unroll=2):
    n = q.shape[0]
    tq = min(tq, seq_len)
    n_k = seq_len // B_TK
    unroll = min(unroll, n_k)
    assert seq_len % tq == 0 and seq_len % B_TK == 0 and n % seq_len == 0 and n_k % unroll == 0
    nq = seq_len // tq
    nv = vt.shape[1]
    return pl.pallas_call(
        functools.partial(_attn_b_kernel, tq=tq, n_k=n_k, unroll=unroll),
        out_shape=jax.ShapeDtypeStruct((n, nv), BF16),
        grid=(n // seq_len, B_HEADS // 2, nq),
        in_specs=[pl.BlockSpec((tq, 2 * LANES), lambda b, p, i: (b * nq + i, p)),
                  pl.BlockSpec((seq_len, 2 * LANES), lambda b, p, i: (b, p)),
                  pl.BlockSpec((n_k, LANES, B_TK), lambda b, p, i: (b, p, 0))],
        out_specs=pl.BlockSpec((tq, LANES), lambda b, p, i: (b * nq + i, p)),
        compiler_params=_cparams(3),
    )(q, k, vt)


def _attn_c_kernel(q_ref, k0, k1, k2, k3, k4, v0, v1, v2, v3, v4, tab_ref, o_ref):
    lo = _low_half()
    nq = q_ref.shape[2]
    for i in range(q_ref.shape[1]):
        for p in range(C_HEADS // 2):
            ps = slice(p * LANES, (p + 1) * LANES)
            kc = jnp.concatenate([r[0, i, :, ps] for r in (k0, k1, k2, k3, k4)], axis=0)
            vc = jnp.concatenate([r[0, i, :, ps] for r in (v0, v1, v2, v3, v4)], axis=0)
            s = _dot_nt(_split_pair(q_ref[0, i, :, ps], lo), kc)
            probs, invs = [], []
            for e in range(2):
                sh = s[e * nq:(e + 1) * nq] + tab_ref[0, 2 * p + e]
                m = jnp.max(sh, axis=-1, keepdims=True)
                pe = jnp.exp2(sh - m)
                invs.append(1.0 / jnp.sum(pe, axis=-1, keepdims=True))
                probs.append(pe.astype(BF16))
            r = jnp.dot(jnp.concatenate(probs, axis=0), vc, preferred_element_type=F32)
            o_ref[0, i, :, ps] = jnp.where(lo, r[:nq] * invs[0], r[nq:] * invs[1]).astype(BF16)


def _c_window_start(rb, rows):
    return jnp.clip(rb - 2, 0, rows // 2 - C_KROWS // 2)


def _c_bias_tables(rpb, rows):
    nrb = rows // C_QROWS
    col = np.arange(GRID_W)
    cs = np.clip(col - C_WIN_COLS // 2, 0, GRID_W - C_WIN_COLS)
    col_ok = (col[None, :] >= cs[:, None]) & (col[None, :] < cs[:, None] + C_WIN_COLS)
    pad = GRID_W - C_WIN_COLS
    padded = jnp.pad(rpb, ((0, 0), (0, 0), (pad, pad)))
    e = jnp.stack([padded[:, :, GRID_W - 1 - qc:2 * GRID_W - 1 - qc] for qc in range(GRID_W)], axis=2) * LOG2_E
    e = jnp.where(jnp.asarray(col_ok)[None, None], e, NEG_INF)
    masked = jnp.full((C_HEADS, GRID_W, GRID_W), NEG_INF, F32)
    tabs = []
    for rb in (0, 1, 2, nrb - 2, nrb - 1):
        w0 = 2 * int(np.clip(rb - 2, 0, rows // 2 - C_KROWS // 2))
        per_qrow = []
        for qrow in range(rb * C_QROWS, (rb + 1) * C_QROWS):
            rs = int(np.clip(qrow - C_WIN_ROWS // 2, 0, rows - C_WIN_ROWS))
            planes = [e[:, krow - qrow + C_WIN_ROWS - 1] if rs <= krow < rs + C_WIN_ROWS else masked
                      for krow in range(w0, w0 + C_KROWS)]
            per_qrow.append(jnp.stack(planes, axis=2))
        tabs.append(jnp.stack(per_qrow, axis=1).reshape(C_HEADS, C_QROWS * GRID_W, C_KROWS * GRID_W))
    return jnp.stack(tabs)


def _attn_c(qkv, rpb, seq_len, group=2):
    n = qkv.shape[0]
    rows = seq_len // GRID_W
    nrb = rows // C_QROWS
    nb = n // seq_len
    assert seq_len % GRID_W == 0 and rows % 2 == 0 and rows >= C_KROWS + 2 and nrb >= 6 and nb % group == 0
    tq = C_QROWS * GRID_W
    nkb = C_KROWS * GRID_W // tq
    nh = C_HEADS * C_HEAD_DIM
    tables = _c_bias_tables(rpb.astype(F32), rows)
    qkv4 = qkv.reshape(nb // group, group, seq_len, qkv.shape[1])

    def kv(col, j):
        return pl.BlockSpec((1, group, tq, nh), lambda rb, b: (b, 0, _c_window_start(rb, rows) + j, col))

    def pattern(rb, b):
        pat = jnp.where(rb < 2, rb, jnp.where(rb >= nrb - 2, rb - (nrb - 2) + 3, 2))
        return (pat, 0, 0, 0)

    out = pl.pallas_call(
        _attn_c_kernel,
        out_shape=jax.ShapeDtypeStruct((nb // group, group, seq_len, nh), BF16),
        grid=(nrb, nb // group),
        in_specs=[pl.BlockSpec((1, group, tq, nh), lambda rb, b: (b, 0, rb, 0))]
                 + [kv(1, j) for j in range(nkb)] + [kv(2, j) for j in range(nkb)]
                 + [pl.BlockSpec((1, C_HEADS, tq, C_KROWS * GRID_W), pattern)],
        out_specs=pl.BlockSpec((1, group, tq, nh), lambda rb, b: (b, 0, rb, 0)),
        compiler_params=_cparams(2),
    )(*([qkv4] * (1 + 2 * nkb)), tables)
    return out.reshape(n, nh)


def _ffn_kernel(*refs, tm, n_chunks, fc, tiles_per_seq, final):
    if final:
        (xm_ref, xp_ref, xn_ref, om_ref, op_ref, on_ref, wo_ref, g_ref, win_ref, cw_ref, wout_ref, fg_ref,
         y_ref, o_scr, h_scr, a_scr) = refs
    else:
        (xm_ref, xp_ref, xn_ref, om_ref, op_ref, on_ref, wo_ref, g_ref, win_ref, cw_ref, wout_ref,
         y_ref, o_scr, h_scr, a_scr) = refs
    t = pl.program_id(0) % tiles_per_seq
    g = g_ref[...]
    nxt = lax.broadcasted_iota(jnp.int32, (BF16_ROWS, 1), 0) < BF16_ROWS // 2
    o_scr[0:tm] = om_ref[...]
    o_scr[tm:] = jnp.where(nxt, on_ref[...].astype(F32), op_ref[...].astype(F32)).astype(BF16)
    mixed = jnp.dot(o_scr[...], wo_ref[...], preferred_element_type=F32)
    x1 = xm_ref[...] + mixed[:tm]
    x1_halo = jnp.where(nxt, xn_ref[...], xp_ref[...]) + mixed[tm:]
    y_ref[...] = x1
    h_scr[0:tm] = _rms(x1, g).astype(BF16)
    inside = jnp.where(nxt, jnp.where(t == tiles_per_seq - 1, 0.0, 1.0), jnp.where(t == 0, 0.0, 1.0))
    h_scr[tm:] = jnp.where(inside > 0.0, _rms(x1_halo, g), 0.0).astype(BF16)
    hext = h_scr[...]
    m = tm + BF16_ROWS
    for c in range(n_chunks):
        u = jnp.dot(hext, win_ref[c], preferred_element_type=F32)
        cw = cw_ref[c]
        z = (cw[3:4] + pltpu.roll(u, 1, 0)[:tm] * cw[0:1] + u[:tm] * cw[1:2]
             + pltpu.roll(u, m - 1, 0)[:tm] * cw[2:3])
        gate, val = z[:, :fc], z[:, fc:]
        a_scr[:, c * fc:(c + 1) * fc] = (gate * (1.0 / (1.0 + jnp.exp(-gate))) * val).astype(BF16)
    y = y_ref[...] + jnp.dot(a_scr[...], wout_ref[...], preferred_element_type=F32)
    if final:
        y = _rms(y, fg_ref[...])
    y_ref[...] = y


def _mix_ffn(x, o, w_o, gain, w_in, cw, w_out, seq_len, final_gain=None, tm=512):
    n, d = x.shape
    n_chunks, _, fc2 = w_in.shape
    assert n % tm == 0 and seq_len % tm == 0 and tm % BF16_ROWS == 0 and o.shape == (n, w_o.shape[0])
    tps = seq_len // tm
    r = tm // BF16_ROWS
    nblk = n // BF16_ROWS
    final = final_gain is not None

    def tile_specs(width):
        return [pl.BlockSpec((tm, width), lambda i: (i, 0)),
                pl.BlockSpec((BF16_ROWS, width), lambda i: (jnp.maximum(i * r - 1, 0), 0)),
                pl.BlockSpec((BF16_ROWS, width), lambda i: (jnp.minimum((i + 1) * r, nblk - 1), 0))]

    in_specs = tile_specs(d) + tile_specs(o.shape[1]) + [
        _resident(w_o.shape), _resident((1, d)), _resident(w_in.shape), _resident(cw.shape), _resident(w_out.shape)]
    args = [x, x, x, o, o, o, w_o, gain.reshape(1, d), w_in, cw, w_out]
    if final:
        in_specs.append(_resident((1, d)))
        args.append(final_gain.reshape(1, d))
    return pl.pallas_call(
        functools.partial(_ffn_kernel, tm=tm, n_chunks=n_chunks, fc=fc2 // 2, tiles_per_seq=tps, final=final),
        out_shape=jax.ShapeDtypeStruct((n, d), F32),
        grid=(n // tm,),
        in_specs=in_specs,
        out_specs=pl.BlockSpec((tm, d), lambda i: (i, 0)),
        scratch_shapes=[pltpu.VMEM((tm + BF16_ROWS, o.shape[1]), BF16), pltpu.VMEM((tm + BF16_ROWS, d), BF16),
                        pltpu.VMEM((tm, w_out.shape[0]), BF16)],
        compiler_params=_cparams(1),
    )(*args)


def _a_head_order():
    order = []
    for p in range(A_KV_HEADS // 2):
        for g in range(A_GROUP):
            order += [(2 * p) * A_GROUP + g, (2 * p + 1) * A_GROUP + g]
    return np.asarray(order)


def _prep_a(w_qkv, w_o):
    cols = (_a_head_order()[:, None] * A_HEAD_DIM + np.arange(A_HEAD_DIM)[None, :]).reshape(-1)
    wq = w_qkv[:, :A_Q_COLS][:, cols] * (A_HEAD_DIM ** -0.5)
    w = jnp.concatenate([wq, w_qkv[:, A_Q_COLS:]], axis=1).astype(BF16)
    return w, w_o[cols, :].astype(BF16)


def _prep_b(w_in, w_uq, w_ukv):
    d = w_in.shape[0]
    nl = B_Q_LORA + B_KV_LORA
    win = jnp.concatenate([w_in[:, :nl], jnp.zeros((d, B_NOPE), F32), w_in[:, nl:],
                           jnp.zeros((d, LANES - B_NOPE - B_ROPE), F32)], axis=1).astype(BF16)
    uq = w_uq.reshape(B_Q_LORA, B_HEADS, B_NOPE + B_ROPE)
    uq = jnp.pad(uq, ((0, 0), (0, 0), (0, LANES - B_NOPE - B_ROPE))).reshape(B_Q_LORA, B_HEADS * LANES)
    ukv = w_ukv.reshape(B_KV_LORA, B_HEADS, B_NOPE + B_V)
    uk = jnp.pad(ukv[:, :, :B_NOPE], ((0, 0), (0, 0), (0, LANES - B_NOPE))).reshape(B_KV_LORA, B_HEADS * LANES)
    uv = ukv[:, :, B_NOPE:].reshape(B_KV_LORA, B_HEADS * B_V)
    return win, uq.astype(BF16), uk.astype(BF16), uv.T.astype(BF16)


def _prep_c(w_qkv):
    nh = C_HEADS * C_HEAD_DIM
    return jnp.concatenate([w_qkv[:, :nh] * (C_HEAD_DIM ** -0.5), w_qkv[:, nh:]], axis=1).astype(BF16)


def _prep_ffn(w_in, conv_w, conv_b, w_out):
    d = w_in.shape[0]
    nc = FFN_DIM // FFN_CHUNK

    def chunked(a):
        r = a.shape[0]
        g = a[:, :FFN_DIM].reshape(r, nc, FFN_CHUNK)
        v = a[:, FFN_DIM:].reshape(r, nc, FFN_CHUNK)
        return jnp.concatenate([g, v], axis=2).transpose(1, 0, 2)

    win = chunked(w_in).astype(BF16)
    cw = chunked(jnp.concatenate([conv_w, conv_b[None, :]], axis=0)).astype(F32)
    return win, cw, w_out.astype(BF16)


def _rope_tables_a(seq_len):
    half = A_HEAD_DIM // 2
    inv = ROPE_THETA ** (-jnp.arange(0, A_HEAD_DIM, 2, dtype=F32) / A_HEAD_DIM)
    ang = jnp.arange(seq_len).astype(F32)[:, None] * inv[None, :]
    cos, sin = jnp.cos(ang), jnp.sin(ang)
    reps = LANES // A_HEAD_DIM
    return jnp.tile(jnp.concatenate([cos, cos], axis=1), (1, reps)), jnp.tile(jnp.concatenate([-sin, sin], axis=1), (1, reps))


def _rope_tables_b(seq_len):
    half = B_ROPE // 2
    inv = ROPE_THETA ** (-jnp.arange(0, B_ROPE, 2, dtype=F32) / B_ROPE)
    ang = jnp.arange(seq_len).astype(F32)[:, None] * inv[None, :]
    cos, sin = jnp.cos(ang), jnp.sin(ang)
    zeros = lambda c: jnp.zeros((seq_len, c), F32)
    ones = lambda c: jnp.ones((seq_len, c), F32)
    tail = LANES - B_NOPE - B_ROPE
    cos_t = jnp.concatenate([ones(B_NOPE), cos, cos, ones(tail)], axis=1)
    sa = jnp.concatenate([zeros(B_NOPE), -sin, zeros(half), zeros(tail)], axis=1)
    sb = jnp.concatenate([zeros(B_NOPE), zeros(half), sin, zeros(tail)], axis=1)
    return cos_t, sa, sb


def _trunk(x, seq_len, norm_mix, norm_ffn, norm_final, layers, ffns, a_sink, b_norms, c_rpb):
    depth = len(layers)
    rope_a = _rope_tables_a(seq_len)
    rope_b = _rope_tables_b(seq_len)
    for i in range(depth):
        kind, j = i % 3, i // 3
        if kind == 0:
            w_qkv, w_o = layers[i]
            qkv = _norm_proj(x, norm_mix[i], w_qkv, seq_len, A_Q_COLS, rope=rope_a, rope_cols=A_Q_COLS + A_KV_COLS)
            o = _attn_a(qkv, a_sink[j], seq_len)
        elif kind == 1:
            w_in, w_uq, w_uk, w_vt, w_o = layers[i]
            q, k, vt = _proj_b(x, norm_mix[i], w_in, b_norms[0][j], b_norms[1][j], w_uq, w_uk, w_vt, rope_b, seq_len)
            o = _attn_b(q, k, vt, seq_len)
        else:
            w_qkv, w_o = layers[i]
            qkv = _norm_proj(x, norm_mix[i], w_qkv, seq_len, C_HEADS * C_HEAD_DIM)
            o = _attn_c(qkv, c_rpb[j], seq_len)
        fin = norm_final if i == depth - 1 else None
        x = _mix_ffn(x, o, w_o, norm_ffn[i], *ffns[i], seq_len, final_gain=fin)
    return x


def kernel(x_prompt, x_sample, norm_mix, norm_ffn, norm_final, a_w_qkv, a_w_o, a_sink, b_w_in, b_q_norm, b_kv_norm, b_w_uq, b_w_ukv, b_w_o, c_w_qkv, c_rpb, c_w_o, f_w_in, f_conv_w, f_conv_b, f_w_out):
    depth = norm_mix.shape[0]
    layers, ffns = [], []
    for i in range(depth):
        kind, j = i % 3, i // 3
        if kind == 0:
            layers.append(_prep_a(a_w_qkv[j], a_w_o[j]))
        elif kind == 1:
            layers.append(_prep_b(b_w_in[j], b_w_uq[j], b_w_ukv[j]) + (b_w_o[j].astype(BF16),))
        else:
            layers.append((_prep_c(c_w_qkv[j]), c_w_o[j].astype(BF16)))
        ffns.append(_prep_ffn(f_w_in[i], f_conv_w[i], f_conv_b[i], f_w_out[i]))
    outs = []
    for x in (x_prompt, x_sample):
        b, t, d = x.shape
        y = _trunk(x.reshape(b * t, d), t, norm_mix, norm_ffn, norm_final, layers, ffns, a_sink,
                   (b_q_norm, b_kv_norm), c_rpb)
        outs.append(y.reshape(b, t, d))
    return tuple(outs)
```

```python
import functools

import numpy as np
import jax
import jax.numpy as jnp
from jax import lax
from jax.experimental import pallas as pl
from jax.experimental.pallas import tpu as pltpu

F32 = jnp.float32
BF16 = jnp.bfloat16

D_MODEL = 1024
NORM_EPS = 1e-6
ROPE_THETA = 10000.0
NEG_INF = -1e30
LOG2_E = 1.4426950408889634

LANES = 128
BF16_ROWS = 16
VMEM_LIMIT = 52 * 1024 * 1024

A_HEADS, A_KV_HEADS, A_HEAD_DIM, A_WINDOW = 16, 4, 64, 128
A_GROUP = A_HEADS // A_KV_HEADS
A_Q_COLS = A_HEADS * A_HEAD_DIM
A_KV_COLS = A_KV_HEADS * A_HEAD_DIM
B_HEADS, B_Q_LORA, B_KV_LORA, B_NOPE, B_ROPE, B_V = 16, 384, 256, 64, 32, 64
B_IN_PAD = B_Q_LORA + B_KV_LORA + LANES
GRID_W, C_HEADS, C_HEAD_DIM, C_WIN_ROWS, C_WIN_COLS = 64, 16, 64, 8, 16
C_QROWS = 2
C_KROWS = 10
FFN_DIM = 2816
FFN_CHUNKS = (512, 512, 512, 512, 512, 256)


def _cparams(n_axes):
    return pltpu.CompilerParams(dimension_semantics=("arbitrary",) * n_axes, vmem_limit_bytes=VMEM_LIMIT)


def _resident(shape):
    zeros = (0,) * len(shape)
    return pl.BlockSpec(shape, lambda *_: zeros, pipeline_mode=pl.Buffered(1))


def _rms(x, g):
    return x * lax.rsqrt(jnp.mean(x * x, axis=-1, keepdims=True) + NORM_EPS) * g


def _low_half():
    return lax.broadcasted_iota(jnp.int32, (1, LANES), 1) < (LANES // 2)


def _split_pair(x, lo):
    zero = jnp.zeros_like(x)
    return jnp.concatenate([jnp.where(lo, x, zero), jnp.where(lo, zero, x)], axis=0)


def _dot_nt(a, b):
    return lax.dot_general(a, b, (((1,), (1,)), ((), ())), preferred_element_type=F32)


def _proj_kernel(*refs, n_out, q_cols, rope_cols, chunk):
    if rope_cols:
        x_ref, g_ref, w_ref, cos_ref, sin_ref, o_ref = refs
    else:
        x_ref, g_ref, w_ref, o_ref = refs
    h = _rms(x_ref[...], g_ref[...]).astype(BF16)
    if rope_cols:
        lane = lax.broadcasted_iota(jnp.int32, (1, LANES), 1)
        first = (lane % A_HEAD_DIM) < (A_HEAD_DIM // 2)
        cos = cos_ref[...]
        sin = sin_ref[...]
    for c0 in range(0, n_out, chunk):
        y = jnp.dot(h, w_ref[:, c0:c0 + chunk], preferred_element_type=F32)
        for s in range(0, chunk, LANES):
            yc = y[:, s:s + LANES]
            if c0 + s < rope_cols:
                rot = jnp.where(first, pltpu.roll(yc, LANES - A_HEAD_DIM // 2, 1),
                                pltpu.roll(yc, A_HEAD_DIM // 2, 1))
                yc = yc * cos + rot * sin
            if c0 + s < q_cols:
                yc = yc * LOG2_E
            o_ref[:, c0 + s:c0 + s + LANES] = yc.astype(BF16)


def _norm_proj(x, gain, w, seq_len, q_cols, rope=None, rope_cols=0, tm=1024, chunk=512):
    n, d = x.shape
    n_out = w.shape[1]
    assert n % tm == 0 and seq_len % tm == 0 and n_out % chunk == 0
    tiles_per_seq = seq_len // tm
    in_specs = [pl.BlockSpec((tm, d), lambda i: (i, 0)), _resident((1, d)), _resident((d, n_out))]
    args = [x, gain.reshape(1, d), w]
    if rope_cols:
        in_specs += [pl.BlockSpec((tm, LANES), lambda i: (i % tiles_per_seq, 0))] * 2
        args += list(rope)
    return pl.pallas_call(
        functools.partial(_proj_kernel, n_out=n_out, q_cols=q_cols, rope_cols=rope_cols, chunk=chunk),
        out_shape=jax.ShapeDtypeStruct((n, n_out), BF16),
        grid=(n // tm,),
        in_specs=in_specs,
        out_specs=pl.BlockSpec((tm, n_out), lambda i: (i, 0)),
        compiler_params=_cparams(1),
    )(*args)


def _attn_a_kernel(sink_ref, q_ref, kp_ref, km_ref, kn_ref, vp_ref, vm_ref, vn_ref, band_ref, o_ref,
                   kbuf, vbuf, *, tq):
    w = A_WINDOW
    i = pl.program_id(1)
    n_i = pl.num_programs(1)
    kbuf[0:w] = kp_ref[...]
    kbuf[w:w + tq] = km_ref[...]
    kbuf[w + tq:] = kn_ref[...]
    vbuf[0:w] = vp_ref[...]
    vbuf[w:w + tq] = vm_ref[...]
    vbuf[w + tq:] = vn_ref[...]
    lo = _low_half()
    kidx = lax.broadcasted_iota(jnp.int32, (1, 3 * w), 1)
    band = band_ref[...]
    n_sub = tq // w
    for j in range(n_sub):
        bias = band
        if j == 0:
            bias = bias + jnp.where(kidx < w, jnp.where(i == 0, NEG_INF, 0.0), 0.0)
        if j == n_sub - 1:
            bias = bias + jnp.where(kidx >= 2 * w, jnp.where(i == n_i - 1, NEG_INF, 0.0), 0.0)
        for p in range(A_KV_HEADS // 2):
            kw = kbuf[j * w:(j + 3) * w, p * LANES:(p + 1) * LANES]
            vw = vbuf[j * w:(j + 3) * w, p * LANES:(p + 1) * LANES]
            qg = jnp.concatenate(
                [q_ref[j * w:(j + 1) * w, (p * A_GROUP + g) * LANES:(p * A_GROUP + g + 1) * LANES]
                 for g in range(A_GROUP)], axis=0)
            s = _dot_nt(_split_pair(qg, lo), kw)
            probs, invs = [], []
            for e in range(2):
                for g in range(A_GROUP):
                    r0 = (e * A_GROUP + g) * w
                    sh = s[r0:r0 + w] + bias
                    sk = sink_ref[(2 * p + e) * A_GROUP + g] * LOG2_E
                    m = jnp.maximum(jnp.max(sh, axis=-1, keepdims=True), sk)
                    pe = jnp.exp2(sh - m)
                    invs.append(1.0 / (jnp.sum(pe, axis=-1, keepdims=True) + jnp.exp2(sk - m)))
                    probs.append(pe.astype(BF16))
            r = jnp.dot(jnp.concatenate(probs, axis=0), vw, preferred_element_type=F32)
            for g in range(A_GROUP):
                o = jnp.where(lo, r[g * w:(g + 1) * w] * invs[g],
                              r[(A_GROUP + g) * w:(A_GROUP + g + 1) * w] * invs[A_GROUP + g])
                c = (p * A_GROUP + g) * LANES
                o_ref[j * w:(j + 1) * w, c:c + LANES] = o.astype(BF16)


def _attn_a(qkv, sink, seq_len, tq=1024):
    n = qkv.shape[0]
    w = A_WINDOW
    tq = min(tq, seq_len)
    assert seq_len % tq == 0 and tq % w == 0 and n % seq_len == 0
    nq, nb, r = seq_len // tq, seq_len // w, tq // w
    kcol, vcol = A_Q_COLS // A_KV_COLS, A_Q_COLS // A_KV_COLS + 1
    rel = np.arange(3 * w)[None, :] - w - np.arange(w)[:, None]
    band = jnp.asarray(np.where(np.abs(rel) <= w, 0.0, NEG_INF), F32)

    def main(col):
        return pl.BlockSpec((tq, A_KV_COLS), lambda b, i: (b * nq + i, col))

    def prev(col):
        return pl.BlockSpec((w, A_KV_COLS), lambda b, i: (b * nb + jnp.maximum(i * r - 1, 0), col))

    def nxt(col):
        return pl.BlockSpec((w, A_KV_COLS), lambda b, i: (b * nb + jnp.minimum((i + 1) * r, nb - 1), col))

    return pl.pallas_call(
        functools.partial(_attn_a_kernel, tq=tq),
        out_shape=jax.ShapeDtypeStruct((n, A_Q_COLS), BF16),
        grid=(n // seq_len, nq),
        in_specs=[pl.BlockSpec(memory_space=pltpu.SMEM),
                  pl.BlockSpec((tq, A_Q_COLS), lambda b, i: (b * nq + i, 0)),
                  prev(kcol), main(kcol), nxt(kcol), prev(vcol), main(vcol), nxt(vcol),
                  _resident((w, 3 * w))],
        out_specs=pl.BlockSpec((tq, A_Q_COLS), lambda b, i: (b * nq + i, 0)),
        scratch_shapes=[pltpu.VMEM((tq + 2 * w, A_KV_COLS), BF16), pltpu.VMEM((tq + 2 * w, A_KV_COLS), BF16)],
        compiler_params=_cparams(2),
    )(sink, qkv, qkv, qkv, qkv, qkv, qkv, qkv, band)


B_TK = 1024


def _proj_b_kernel(x_ref, g_ref, win_ref, qn_ref, kvn_ref, wuq_ref, wuk_ref, wvt_ref, cos_ref, sa_ref, sb_ref,
                   q_ref, k_ref, vt_ref):
    h = _rms(x_ref[...], g_ref[...]).astype(BF16)
    c = jnp.dot(h, win_ref[...], preferred_element_type=F32)
    cq = _rms(c[:, :B_Q_LORA], qn_ref[...]).astype(BF16)
    ckv = _rms(c[:, B_Q_LORA:B_Q_LORA + B_KV_LORA], kvn_ref[...]).astype(BF16)
    cos, sa, sb = cos_ref[...], sa_ref[...], sb_ref[...]
    half = B_ROPE // 2

    def rope(y):
        return y * cos + pltpu.roll(y, LANES - half, 1) * sa + pltpu.roll(y, half, 1) * sb

    kr = rope(c[:, B_Q_LORA + B_KV_LORA:])
    scale = (B_NOPE + B_ROPE) ** -0.5 * LOG2_E
    for hh in range(B_HEADS):
        sl = slice(hh * LANES, (hh + 1) * LANES)
        q = jnp.dot(cq, wuq_ref[:, sl], preferred_element_type=F32)
        q_ref[:, sl] = (rope(q) * scale).astype(BF16)
        k = jnp.dot(ckv, wuk_ref[:, sl], preferred_element_type=F32)
        k_ref[:, sl] = (k + kr).astype(BF16)
    vt_ref[0] = _dot_nt(wvt_ref[...], ckv).astype(BF16)


def _proj_b(x, gain, w_in, q_norm, kv_norm, w_uq, w_uk, w_vt, tables, seq_len):
    n, d = x.shape
    tm = B_TK
    assert n % tm == 0 and seq_len % tm == 0
    tps = seq_len // tm
    nq = B_HEADS * LANES
    nv = B_HEADS * B_V
    tab = pl.BlockSpec((tm, LANES), lambda i: (i % tps, 0))
    row = lambda c: pl.BlockSpec((tm, c), lambda i: (i, 0))
    return pl.pallas_call(
        _proj_b_kernel,
        out_shape=(jax.ShapeDtypeStruct((n, nq), BF16), jax.ShapeDtypeStruct((n, nq), BF16),
                   jax.ShapeDtypeStruct((n // tm, nv, tm), BF16)),
        grid=(n // tm,),
        in_specs=[row(d), _resident((1, d)), _resident((d, B_IN_PAD)), _resident((1, B_Q_LORA)),
                  _resident((1, B_KV_LORA)), _resident((B_Q_LORA, nq)), _resident((B_KV_LORA, nq)),
                  _resident((nv, B_KV_LORA)), tab, tab, tab],
        out_specs=(row(nq), row(nq), pl.BlockSpec((1, nv, tm), lambda i: (i, 0, 0))),
        compiler_params=_cparams(1),
    )(x, gain.reshape(1, d), w_in, q_norm.reshape(1, -1), kv_norm.reshape(1, -1), w_uq, w_uk, w_vt, *tables)


def _attn_b_kernel(q_ref, k_ref, vt_ref, o_ref, *, tq, n_k, unroll):
    row = lax.broadcasted_iota(jnp.int32, (LANES, 1), 0)
    row_lo = row < (LANES // 2)
    ones0 = jnp.where(row == LANES // 2, 1.0, 0.0).astype(BF16)
    ones1 = jnp.where(row == 0, 1.0, 0.0).astype(BF16)
    q0, q1 = q_ref[:, :LANES], q_ref[:, LANES:]

    def update(kc, q, m_old):
        s = _dot_nt(kc, q)
        m_new = jnp.maximum(m_old, jnp.max(s, axis=0, keepdims=True))
        return m_new, jnp.exp2(m_old - m_new), jnp.exp2(s - m_new).astype(BF16)

    def body(j, carry):
        m0, m1, acc0, acc1 = carry
        r0 = pl.multiple_of(j * B_TK, B_TK)
        kc = k_ref[pl.ds(r0, B_TK), :]
        m0, a0, p0 = update(kc[:, :LANES], q0, m0)
        m1, a1, p1 = update(kc[:, LANES:], q1, m1)
        vt = vt_ref[j]
        acc0 = acc0 * a0 + jnp.dot(jnp.where(row_lo, vt, ones0), p0, preferred_element_type=F32)
        acc1 = acc1 * a1 + jnp.dot(jnp.where(row_lo, ones1, vt), p1, preferred_element_type=F32)
        return m0, m1, acc0, acc1

    vec = jnp.full((1, tq), NEG_INF, F32)
    zero = jnp.zeros((LANES, tq), F32)
    _, _, acc0, acc1 = lax.fori_loop(0, n_k, body, (vec, vec, zero, zero), unroll=unroll)
    out = jnp.where(row_lo, acc0 * (1.0 / acc0[LANES // 2:LANES // 2 + 1]), acc1 * (1.0 / acc1[0:1]))
    o_ref[...] = out.T.astype(BF16)


def _attn_b(q, k, vt, seq_len, tq=1024, unroll=2):
    n = q.shape[0]
    tq = min(tq, seq_len)
    n_k = seq_len // B_TK
    unroll = min(unroll, n_k)
    assert seq_len % tq == 0 and seq_len % B_TK == 0 and n % seq_len == 0 and n_k % unroll == 0
    nq = seq_len // tq
    nv = vt.shape[1]
    return pl.pallas_call(
        functools.partial(_attn_b_kernel, tq=tq, n_k=n_k, unroll=unroll),
        out_shape=jax.ShapeDtypeStruct((n, nv), BF16),
        grid=(n // seq_len, B_HEADS // 2, nq),
        in_specs=[pl.BlockSpec((tq, 2 * LANES), lambda b, p, i: (b * nq + i, p)),
                  pl.BlockSpec((seq_len, 2 * LANES), lambda b, p, i: (b, p)),
                  pl.BlockSpec((n_k, LANES, B_TK), lambda b, p, i: (b, p, 0))],
        out_specs=pl.BlockSpec((tq, LANES), lambda b, p, i: (b * nq + i, p)),
        compiler_params=_cparams(3),
    )(q, k, vt)


def _attn_c_kernel(q_ref, k0, k1, k2, k3, k4, v0, v1, v2, v3, v4, tab_ref, o_ref):
    lo = _low_half()
    nq = q_ref.shape[2]
    ones = jnp.ones((k0.shape[2] * 5, LANES), BF16)
    for i in range(q_ref.shape[1]):
        for p in range(C_HEADS // 2):
            ps = slice(p * LANES, (p + 1) * LANES)
            kc = jnp.concatenate([r[0, i, :, ps] for r in (k0, k1, k2, k3, k4)], axis=0)
            vc = jnp.concatenate([r[0, i, :, ps] for r in (v0, v1, v2, v3, v4)], axis=0)
            s = _dot_nt(_split_pair(q_ref[0, i, :, ps], lo), kc)
            probs = []
            for e in range(2):
                sh = s[e * nq:(e + 1) * nq] + tab_ref[0, 2 * p + e]
                probs.append(jnp.exp2(sh - jnp.max(sh, axis=-1, keepdims=True)).astype(BF16))
            r = jnp.dot(jnp.concatenate(probs, axis=0), jnp.concatenate([vc, ones], axis=1),
                        preferred_element_type=F32)
            r = r[:, :LANES] * (1.0 / r[:, LANES:])
            o_ref[0, i, :, ps] = jnp.where(lo, r[:nq], r[nq:]).astype(BF16)


def _c_window_start(rb, rows):
    return jnp.clip(rb - 2, 0, rows // 2 - C_KROWS // 2)


def _c_bias_tables(rpb, rows):
    nrb = rows // C_QROWS
    col = np.arange(GRID_W)
    cs = np.clip(col - C_WIN_COLS // 2, 0, GRID_W - C_WIN_COLS)
    col_ok = (col[None, :] >= cs[:, None]) & (col[None, :] < cs[:, None] + C_WIN_COLS)
    pad = GRID_W - C_WIN_COLS
    padded = jnp.pad(rpb, ((0, 0), (0, 0), (pad, pad)))
    e = jnp.stack([padded[:, :, GRID_W - 1 - qc:2 * GRID_W - 1 - qc] for qc in range(GRID_W)], axis=2) * LOG2_E
    e = jnp.where(jnp.asarray(col_ok)[None, None], e, NEG_INF)
    masked = jnp.full((C_HEADS, GRID_W, GRID_W), NEG_INF, F32)
    tabs = []
    for rb in (0, 1, 2, nrb - 2, nrb - 1):
        w0 = 2 * int(np.clip(rb - 2, 0, rows // 2 - C_KROWS // 2))
        per_qrow = []
        for qrow in range(rb * C_QROWS, (rb + 1) * C_QROWS):
            rs = int(np.clip(qrow - C_WIN_ROWS // 2, 0, rows - C_WIN_ROWS))
            planes = [e[:, krow - qrow + C_WIN_ROWS - 1] if rs <= krow < rs + C_WIN_ROWS else masked
                      for krow in range(w0, w0 + C_KROWS)]
            per_qrow.append(jnp.stack(planes, axis=2))
        tabs.append(jnp.stack(per_qrow, axis=1).reshape(C_HEADS, C_QROWS * GRID_W, C_KROWS * GRID_W))
    return jnp.stack(tabs)


def _attn_c(qkv, rpb, seq_len, group=2):
    n = qkv.shape[0]
    rows = seq_len // GRID_W
    nrb = rows // C_QROWS
    nb = n // seq_len
    assert seq_len % GRID_W == 0 and rows % 2 == 0 and rows >= C_KROWS + 2 and nrb >= 6 and nb % group == 0
    tq = C_QROWS * GRID_W
    nkb = C_KROWS * GRID_W // tq
    nh = C_HEADS * C_HEAD_DIM
    tables = _c_bias_tables(rpb.astype(F32), rows)
    qkv4 = qkv.reshape(nb // group, group, seq_len, qkv.shape[1])

    def kv(col, j):
        return pl.BlockSpec((1, group, tq, nh), lambda rb, b: (b, 0, _c_window_start(rb, rows) + j, col))

    def pattern(rb, b):
        pat = jnp.where(rb < 2, rb, jnp.where(rb >= nrb - 2, rb - (nrb - 2) + 3, 2))
        return (pat, 0, 0, 0)

    out = pl.pallas_call(
        _attn_c_kernel,
        out_shape=jax.ShapeDtypeStruct((nb // group, group, seq_len, nh), BF16),
        grid=(nrb, nb // group),
        in_specs=[pl.BlockSpec((1, group, tq, nh), lambda rb, b: (b, 0, rb, 0))]
                 + [kv(1, j) for j in range(nkb)] + [kv(2, j) for j in range(nkb)]
                 + [pl.BlockSpec((1, C_HEADS, tq, C_KROWS * GRID_W), pattern)],
        out_specs=pl.BlockSpec((1, group, tq, nh), lambda rb, b: (b, 0, rb, 0)),
        compiler_params=_cparams(2),
    )(*([qkv4] * (1 + 2 * nkb)), tables)
    return out.reshape(n, nh)


def _ffn_kernel(*refs, tm, tiles_per_seq, final):
    if final:
        (xm_ref, xp_ref, xn_ref, om_ref, op_ref, on_ref, wo_ref, g_ref, win_ref, cw_ref, wout_ref, fg_ref,
         y_ref, o_scr, h_scr, a_scr) = refs
    else:
        (xm_ref, xp_ref, xn_ref, om_ref, op_ref, on_ref, wo_ref, g_ref, win_ref, cw_ref, wout_ref,
         y_ref, o_scr, h_scr, a_scr) = refs
    t = pl.program_id(0) % tiles_per_seq
    g = g_ref[...]
    nxt = lax.broadcasted_iota(jnp.int32, (BF16_ROWS, 1), 0) < BF16_ROWS // 2
    o_scr[0:tm] = om_ref[...]
    o_scr[tm:] = jnp.where(nxt, on_ref[...].astype(F32), op_ref[...].astype(F32)).astype(BF16)
    mixed = jnp.dot(o_scr[...], wo_ref[...], preferred_element_type=F32)
    x1 = xm_ref[...] + mixed[:tm]
    x1_halo = jnp.where(nxt, xn_ref[...], xp_ref[...]) + mixed[tm:]
    y_ref[...] = x1
    h_scr[0:tm] = _rms(x1, g).astype(BF16)
    inside = jnp.where(nxt, jnp.where(t == tiles_per_seq - 1, 0.0, 1.0), jnp.where(t == 0, 0.0, 1.0))
    h_scr[tm:] = jnp.where(inside > 0.0, _rms(x1_halo, g), 0.0).astype(BF16)
    hext = h_scr[...]
    m = tm + BF16_ROWS
    off = 0
    for fc in FFN_CHUNKS:
        cols = slice(2 * off, 2 * (off + fc))
        u = jnp.dot(hext, win_ref[:, cols], preferred_element_type=F32)
        cw = cw_ref[:, cols]
        z = (cw[3:4] + pltpu.roll(u, 1, 0)[:tm] * cw[0:1] + u[:tm] * cw[1:2]
             + pltpu.roll(u, m - 1, 0)[:tm] * cw[2:3])
        gate, val = z[:, :fc], z[:, fc:]
        a_scr[:, off:off + fc] = (gate * (1.0 / (1.0 + jnp.exp(-gate))) * val).astype(BF16)
        off += fc
    y = y_ref[...] + jnp.dot(a_scr[...], wout_ref[...], preferred_element_type=F32)
    if final:
        y = _rms(y, fg_ref[...])
    y_ref[...] = y


def _mix_ffn(x, o, w_o, gain, w_in, cw, w_out, seq_len, final_gain=None, tm=512):
    n, d = x.shape
    assert n % tm == 0 and seq_len % tm == 0 and tm % BF16_ROWS == 0 and o.shape == (n, w_o.shape[0])
    tps = seq_len // tm
    r = tm // BF16_ROWS
    nblk = n // BF16_ROWS
    final = final_gain is not None

    def tile_specs(width):
        return [pl.BlockSpec((tm, width), lambda i: (i, 0)),
                pl.BlockSpec((BF16_ROWS, width), lambda i: (jnp.maximum(i * r - 1, 0), 0)),
                pl.BlockSpec((BF16_ROWS, width), lambda i: (jnp.minimum((i + 1) * r, nblk - 1), 0))]

    in_specs = tile_specs(d) + tile_specs(o.shape[1]) + [
        _resident(w_o.shape), _resident((1, d)), _resident(w_in.shape), _resident(cw.shape), _resident(w_out.shape)]
    args = [x, x, x, o, o, o, w_o, gain.reshape(1, d), w_in, cw, w_out]
    if final:
        in_specs.append(_resident((1, d)))
        args.append(final_gain.reshape(1, d))
    return pl.pallas_call(
        functools.partial(_ffn_kernel, tm=tm, tiles_per_seq=tps, final=final),
        out_shape=jax.ShapeDtypeStruct((n, d), F32),
        grid=(n // tm,),
        in_specs=in_specs,
        out_specs=pl.BlockSpec((tm, d), lambda i: (i, 0)),
        scratch_shapes=[pltpu.VMEM((tm + BF16_ROWS, o.shape[1]), BF16), pltpu.VMEM((tm + BF16_ROWS, d), BF16),
                        pltpu.VMEM((tm, w_out.shape[0]), BF16)],
        compiler_params=_cparams(1),
    )(*args)


def _a_head_order():
    order = []
    for p in range(A_KV_HEADS // 2):
        for g in range(A_GROUP):
            order += [(2 * p) * A_GROUP + g, (2 * p + 1) * A_GROUP + g]
    return np.asarray(order)


def _prep_a(w_qkv, w_o):
    cols = (_a_head_order()[:, None] * A_HEAD_DIM + np.arange(A_HEAD_DIM)[None, :]).reshape(-1)
    wq = w_qkv[:, :A_Q_COLS][:, cols] * (A_HEAD_DIM ** -0.5)
    w = jnp.concatenate([wq, w_qkv[:, A_Q_COLS:]], axis=1).astype(BF16)
    return w, w_o[cols, :].astype(BF16)


def _prep_b(w_in, w_uq, w_ukv):
    d = w_in.shape[0]
    nl = B_Q_LORA + B_KV_LORA
    win = jnp.concatenate([w_in[:, :nl], jnp.zeros((d, B_NOPE), F32), w_in[:, nl:],
                           jnp.zeros((d, LANES - B_NOPE - B_ROPE), F32)], axis=1).astype(BF16)
    uq = w_uq.reshape(B_Q_LORA, B_HEADS, B_NOPE + B_ROPE)
    uq = jnp.pad(uq, ((0, 0), (0, 0), (0, LANES - B_NOPE - B_ROPE))).reshape(B_Q_LORA, B_HEADS * LANES)
    ukv = w_ukv.reshape(B_KV_LORA, B_HEADS, B_NOPE + B_V)
    uk = jnp.pad(ukv[:, :, :B_NOPE], ((0, 0), (0, 0), (0, LANES - B_NOPE))).reshape(B_KV_LORA, B_HEADS * LANES)
    uv = ukv[:, :, B_NOPE:].reshape(B_KV_LORA, B_HEADS * B_V)
    return win, uq.astype(BF16), uk.astype(BF16), uv.T.astype(BF16)


def _prep_c(w_qkv):
    nh = C_HEADS * C_HEAD_DIM
    return jnp.concatenate([w_qkv[:, :nh] * (C_HEAD_DIM ** -0.5), w_qkv[:, nh:]], axis=1).astype(BF16)


def _prep_ffn(w_in, conv_w, conv_b, w_out):
    assert sum(FFN_CHUNKS) == FFN_DIM

    def chunked(a):
        parts, off = [], 0
        for fc in FFN_CHUNKS:
            parts += [a[:, off:off + fc], a[:, FFN_DIM + off:FFN_DIM + off + fc]]
            off += fc
        return jnp.concatenate(parts, axis=1)

    win = chunked(w_in).astype(BF16)
    cw = chunked(jnp.concatenate([conv_w, conv_b[None, :]], axis=0)).astype(F32)
    return win, cw, w_out.astype(BF16)


def _rope_tables_a(seq_len):
    half = A_HEAD_DIM // 2
    inv = ROPE_THETA ** (-jnp.arange(0, A_HEAD_DIM, 2, dtype=F32) / A_HEAD_DIM)
    ang = jnp.arange(seq_len).astype(F32)[:, None] * inv[None, :]
    cos, sin = jnp.cos(ang), jnp.sin(ang)
    reps = LANES // A_HEAD_DIM
    return jnp.tile(jnp.concatenate([cos, cos], axis=1), (1, reps)), jnp.tile(jnp.concatenate([-sin, sin], axis=1), (1, reps))


def _rope_tables_b(seq_len):
    half = B_ROPE // 2
    inv = ROPE_THETA ** (-jnp.arange(0, B_ROPE, 2, dtype=F32) / B_ROPE)
    ang = jnp.arange(seq_len).astype(F32)[:, None] * inv[None, :]
    cos, sin = jnp.cos(ang), jnp.sin(ang)
    zeros = lambda c: jnp.zeros((seq_len, c), F32)
    ones = lambda c: jnp.ones((seq_len, c), F32)
    tail = LANES - B_NOPE - B_ROPE
    cos_t = jnp.concatenate([ones(B_NOPE), cos, cos, ones(tail)], axis=1)
    sa = jnp.concatenate([zeros(B_NOPE), -sin, zeros(half), zeros(tail)], axis=1)
    sb = jnp.concatenate([zeros(B_NOPE), zeros(half), sin, zeros(tail)], axis=1)
    return cos_t, sa, sb


def _trunk(x, seq_len, norm_mix, norm_ffn, norm_final, layers, ffns, a_sink, b_norms, c_rpb):
    depth = len(layers)
    rope_a = _rope_tables_a(seq_len)
    rope_b = _rope_tables_b(seq_len)
    for i in range(depth):
        kind, j = i % 3, i // 3
        if kind == 0:
            w_qkv, w_o = layers[i]
            qkv = _norm_proj(x, norm_mix[i], w_qkv, seq_len, A_Q_COLS, rope=rope_a, rope_cols=A_Q_COLS + A_KV_COLS)
            o = _attn_a(qkv, a_sink[j], seq_len)
        elif kind == 1:
            w_in, w_uq, w_uk, w_vt, w_o = layers[i]
            q, k, vt = _proj_b(x, norm_mix[i], w_in, b_norms[0][j], b_norms[1][j], w_uq, w_uk, w_vt, rope_b, seq_len)
            o = _attn_b(q, k, vt, seq_len)
        else:
            w_qkv, w_o = layers[i]
            qkv = _norm_proj(x, norm_mix[i], w_qkv, seq_len, C_HEADS * C_HEAD_DIM)
            o = _attn_c(qkv, c_rpb[j], seq_len)
        fin = norm_final if i == depth - 1 else None
        x = _mix_ffn(x, o, w_o, norm_ffn[i], *ffns[i], seq_len, final_gain=fin)
    return x


def kernel(x_prompt, x_sample, norm_mix, norm_ffn, norm_final, a_w_qkv, a_w_o, a_sink, b_w_in, b_q_norm, b_kv_norm, b_w_uq, b_w_ukv, b_w_o, c_w_qkv, c_rpb, c_w_o, f_w_in, f_conv_w, f_conv_b, f_w_out):
    depth = norm_mix.shape[0]
    layers, ffns = [], []
    for i in range(depth):
        kind, j = i % 3, i // 3
        if kind == 0:
            layers.append(_prep_a(a_w_qkv[j], a_w_o[j]))
        elif kind == 1:
            layers.append(_prep_b(b_w_in[j], b_w_uq[j], b_w_ukv[j]) + (b_w_o[j].astype(BF16),))
        else:
            layers.append((_prep_c(c_w_qkv[j]), c_w_o[j].astype(BF16)))
        ffns.append(_prep_ffn(f_w_in[i], f_conv_w[i], f_conv_b[i], f_w_out[i]))
    outs = []
    for x in (x_prompt, x_sample):
        b, t, d = x.shape
        y = _trunk(x.reshape(b * t, d), t, norm_mix, norm_ffn, norm_final, layers, ffns, a_sink,
                   (b_q_norm, b_kv_norm), c_rpb)
        outs.append(y.reshape(b, t, d))
    return tuple(outs)
```

```python
import functools

import numpy as np
import jax
import jax.numpy as jnp
from jax import lax
from jax.experimental import pallas as pl
from jax.experimental.pallas import tpu as pltpu

F32 = jnp.float32
BF16 = jnp.bfloat16

D_MODEL = 1024
NORM_EPS = 1e-6
ROPE_THETA = 10000.0
NEG_INF = -1e30
LOG2_E = 1.4426950408889634

LANES = 128
BF16_ROWS = 16
VMEM_LIMIT = 52 * 1024 * 1024

A_HEADS, A_KV_HEADS, A_HEAD_DIM, A_WINDOW = 16, 4, 64, 128
A_GROUP = A_HEADS // A_KV_HEADS
A_Q_COLS = A_HEADS * A_HEAD_DIM
A_KV_COLS = A_KV_HEADS * A_HEAD_DIM
B_HEADS, B_Q_LORA, B_KV_LORA, B_NOPE, B_ROPE, B_V = 16, 384, 256, 64, 32, 64
B_IN_PAD = B_Q_LORA + B_KV_LORA + LANES
GRID_W, C_HEADS, C_HEAD_DIM, C_WIN_ROWS, C_WIN_COLS = 64, 16, 64, 8, 16
C_QROWS = 2
C_KROWS = 10
FFN_DIM = 2816
FFN_CHUNKS = (512, 512, 512, 512, 512, 256)


def _cparams(n_axes):
    return pltpu.CompilerParams(dimension_semantics=("arbitrary",) * n_axes, vmem_limit_bytes=VMEM_LIMIT)


def _resident(shape):
    zeros = (0,) * len(shape)
    return pl.BlockSpec(shape, lambda *_: zeros, pipeline_mode=pl.Buffered(1))


def _rms(x, g):
    return x * lax.rsqrt(jnp.mean(x * x, axis=-1, keepdims=True) + NORM_EPS) * g


def _low_half():
    return lax.broadcasted_iota(jnp.int32, (1, LANES), 1) < (LANES // 2)


def _split_pair(x, lo):
    zero = jnp.zeros_like(x)
    return jnp.concatenate([jnp.where(lo, x, zero), jnp.where(lo, zero, x)], axis=0)


def _dot_nt(a, b):
    return lax.dot_general(a, b, (((1,), (1,)), ((), ())), preferred_element_type=F32)


def _proj_kernel(*refs, n_out, q_cols, rope_cols, chunk):
    if rope_cols:
        x_ref, g_ref, w_ref, cos_ref, sin_ref, o_ref = refs
    else:
        x_ref, g_ref, w_ref, o_ref = refs
    h = _rms(x_ref[...], g_ref[...]).astype(BF16)
    if rope_cols:
        lane = lax.broadcasted_iota(jnp.int32, (1, LANES), 1)
        first = (lane % A_HEAD_DIM) < (A_HEAD_DIM // 2)
        cos = cos_ref[...]
        sin = sin_ref[...]
    for c0 in range(0, n_out, chunk):
        y = jnp.dot(h, w_ref[:, c0:c0 + chunk], preferred_element_type=F32)
        for s in range(0, chunk, LANES):
            yc = y[:, s:s + LANES]
            if c0 + s < rope_cols:
                rot = jnp.where(first, pltpu.roll(yc, LANES - A_HEAD_DIM // 2, 1),
                                pltpu.roll(yc, A_HEAD_DIM // 2, 1))
                yc = yc * cos + rot * sin
            if c0 + s < q_cols:
                yc = yc * LOG2_E
            o_ref[:, c0 + s:c0 + s + LANES] = yc.astype(BF16)


def _norm_proj(x, gain, w, seq_len, q_cols, rope=None, rope_cols=0, tm=1024, chunk=512):
    n, d = x.shape
    n_out = w.shape[1]
    assert n % tm == 0 and seq_len % tm == 0 and n_out % chunk == 0
    tiles_per_seq = seq_len // tm
    in_specs = [pl.BlockSpec((tm, d), lambda i: (i, 0)), _resident((1, d)), _resident((d, n_out))]
    args = [x, gain.reshape(1, d), w]
    if rope_cols:
        in_specs += [pl.BlockSpec((tm, LANES), lambda i: (i % tiles_per_seq, 0))] * 2
        args += list(rope)
    return pl.pallas_call(
        functools.partial(_proj_kernel, n_out=n_out, q_cols=q_cols, rope_cols=rope_cols, chunk=chunk),
        out_shape=jax.ShapeDtypeStruct((n, n_out), BF16),
        grid=(n // tm,),
        in_specs=in_specs,
        out_specs=pl.BlockSpec((tm, n_out), lambda i: (i, 0)),
        compiler_params=_cparams(1),
    )(*args)


def _attn_a_kernel(sink_ref, q_ref, kp_ref, km_ref, kn_ref, vp_ref, vm_ref, vn_ref, band_ref, o_ref,
                   kbuf, vbuf, *, tq):
    w = A_WINDOW
    i = pl.program_id(1)
    n_i = pl.num_programs(1)
    kbuf[0:w] = kp_ref[...]
    kbuf[w:w + tq] = km_ref[...]
    kbuf[w + tq:] = kn_ref[...]
    vbuf[0:w] = vp_ref[...]
    vbuf[w:w + tq] = vm_ref[...]
    vbuf[w + tq:] = vn_ref[...]
    lo = _low_half()
    kidx = lax.broadcasted_iota(jnp.int32, (1, 3 * w), 1)
    band = band_ref[...]
    n_sub = tq // w
    for j in range(n_sub):
        bias = band
        if j == 0:
            bias = bias + jnp.where(kidx < w, jnp.where(i == 0, NEG_INF, 0.0), 0.0)
        if j == n_sub - 1:
            bias = bias + jnp.where(kidx >= 2 * w, jnp.where(i == n_i - 1, NEG_INF, 0.0), 0.0)
        for p in range(A_KV_HEADS // 2):
            kw = kbuf[j * w:(j + 3) * w, p * LANES:(p + 1) * LANES]
            vw = vbuf[j * w:(j + 3) * w, p * LANES:(p + 1) * LANES]
            qg = jnp.concatenate(
                [q_ref[j * w:(j + 1) * w, (p * A_GROUP + g) * LANES:(p * A_GROUP + g + 1) * LANES]
                 for g in range(A_GROUP)], axis=0)
            s = _dot_nt(_split_pair(qg, lo), kw)
            probs, invs = [], []
            for e in range(2):
                for g in range(A_GROUP):
                    r0 = (e * A_GROUP + g) * w
                    sh = s[r0:r0 + w] + bias
                    sk = sink_ref[(2 * p + e) * A_GROUP + g] * LOG2_E
                    m = jnp.maximum(jnp.max(sh, axis=-1, keepdims=True), sk)
                    pe = jnp.exp2(sh - m)
                    invs.append(1.0 / (jnp.sum(pe, axis=-1, keepdims=True) + jnp.exp2(sk - m)))
                    probs.append(pe.astype(BF16))
            r = jnp.dot(jnp.concatenate(probs, axis=0), vw, preferred_element_type=F32)
            for g in range(A_GROUP):
                o = jnp.where(lo, r[g * w:(g + 1) * w] * invs[g],
                              r[(A_GROUP + g) * w:(A_GROUP + g + 1) * w] * invs[A_GROUP + g])
                c = (p * A_GROUP + g) * LANES
                o_ref[j * w:(j + 1) * w, c:c + LANES] = o.astype(BF16)


def _attn_a(qkv, sink, seq_len, tq=1024):
    n = qkv.shape[0]
    w = A_WINDOW
    tq = min(tq, seq_len)
    assert seq_len % tq == 0 and tq % w == 0 and n % seq_len == 0
    nq, nb, r = seq_len // tq, seq_len // w, tq // w
    kcol, vcol = A_Q_COLS // A_KV_COLS, A_Q_COLS // A_KV_COLS + 1
    rel = np.arange(3 * w)[None, :] - w - np.arange(w)[:, None]
    band = jnp.asarray(np.where(np.abs(rel) <= w, 0.0, NEG_INF), F32)

    def main(col):
        return pl.BlockSpec((tq, A_KV_COLS), lambda b, i: (b * nq + i, col))

    def prev(col):
        return pl.BlockSpec((w, A_KV_COLS), lambda b, i: (b * nb + jnp.maximum(i * r - 1, 0), col))

    def nxt(col):
        return pl.BlockSpec((w, A_KV_COLS), lambda b, i: (b * nb + jnp.minimum((i + 1) * r, nb - 1), col))

    return pl.pallas_call(
        functools.partial(_attn_a_kernel, tq=tq),
        out_shape=jax.ShapeDtypeStruct((n, A_Q_COLS), BF16),
        grid=(n // seq_len, nq),
        in_specs=[pl.BlockSpec(memory_space=pltpu.SMEM),
                  pl.BlockSpec((tq, A_Q_COLS), lambda b, i: (b * nq + i, 0)),
                  prev(kcol), main(kcol), nxt(kcol), prev(vcol), main(vcol), nxt(vcol),
                  _resident((w, 3 * w))],
        out_specs=pl.BlockSpec((tq, A_Q_COLS), lambda b, i: (b * nq + i, 0)),
        scratch_shapes=[pltpu.VMEM((tq + 2 * w, A_KV_COLS), BF16), pltpu.VMEM((tq + 2 * w, A_KV_COLS), BF16)],
        compiler_params=_cparams(2),
    )(sink, qkv, qkv, qkv, qkv, qkv, qkv, qkv, band)


B_TK = 1024


def _proj_b_kernel(x_ref, g_ref, win_ref, qn_ref, kvn_ref, wuq_ref, wuk_ref, wvt_ref, cos_ref, sa_ref, sb_ref,
                   q_ref, k_ref, vt_ref):
    h = _rms(x_ref[...], g_ref[...]).astype(BF16)
    c = jnp.dot(h, win_ref[...], preferred_element_type=F32)
    cq = _rms(c[:, :B_Q_LORA], qn_ref[...]).astype(BF16)
    ckv = _rms(c[:, B_Q_LORA:B_Q_LORA + B_KV_LORA], kvn_ref[...]).astype(BF16)
    cos, sa, sb = cos_ref[...], sa_ref[...], sb_ref[...]
    half = B_ROPE // 2

    def rope(y):
        return y * cos + pltpu.roll(y, LANES - half, 1) * sa + pltpu.roll(y, half, 1) * sb

    kr = rope(c[:, B_Q_LORA + B_KV_LORA:])
    scale = (B_NOPE + B_ROPE) ** -0.5 * LOG2_E
    for hh in range(B_HEADS):
        sl = slice(hh * LANES, (hh + 1) * LANES)
        q = jnp.dot(cq, wuq_ref[:, sl], preferred_element_type=F32)
        q_ref[:, sl] = (rope(q) * scale).astype(BF16)
        k = jnp.dot(ckv, wuk_ref[:, sl], preferred_element_type=F32)
        k_ref[:, sl] = (k + kr).astype(BF16)
    vt_ref[0] = _dot_nt(wvt_ref[...], ckv).astype(BF16)


def _proj_b(x, gain, w_in, q_norm, kv_norm, w_uq, w_uk, w_vt, tables, seq_len):
    n, d = x.shape
    tm = B_TK
    assert n % tm == 0 and seq_len % tm == 0
    tps = seq_len // tm
    nq = B_HEADS * LANES
    nv = B_HEADS * B_V
    tab = pl.BlockSpec((tm, LANES), lambda i: (i % tps, 0))
    row = lambda c: pl.BlockSpec((tm, c), lambda i: (i, 0))
    return pl.pallas_call(
        _proj_b_kernel,
        out_shape=(jax.ShapeDtypeStruct((n, nq), BF16), jax.ShapeDtypeStruct((n, nq), BF16),
                   jax.ShapeDtypeStruct((n // tm, nv, tm), BF16)),
        grid=(n // tm,),
        in_specs=[row(d), _resident((1, d)), _resident((d, B_IN_PAD)), _resident((1, B_Q_LORA)),
                  _resident((1, B_KV_LORA)), _resident((B_Q_LORA, nq)), _resident((B_KV_LORA, nq)),
                  _resident((nv, B_KV_LORA)), tab, tab, tab],
        out_specs=(row(nq), row(nq), pl.BlockSpec((1, nv, tm), lambda i: (i, 0, 0))),
        compiler_params=_cparams(1),
    )(x, gain.reshape(1, d), w_in, q_norm.reshape(1, -1), kv_norm.reshape(1, -1), w_uq, w_uk, w_vt, *tables)


def _attn_b_kernel(q_ref, k_ref, vt_ref, o_ref, *, tq, n_k, unroll):
    row = lax.broadcasted_iota(jnp.int32, (LANES, 1), 0)
    row_lo = row < (LANES // 2)
    ones0 = jnp.where(row == LANES // 2, 1.0, 0.0).astype(BF16)
    ones1 = jnp.where(row == 0, 1.0, 0.0).astype(BF16)
    q0, q1 = q_ref[:, :LANES], q_ref[:, LANES:]

    def update(kc, q, m_old):
        s = _dot_nt(kc, q)
        m_new = jnp.maximum(m_old, jnp.max(s, axis=0, keepdims=True))
        return m_new, jnp.exp2(m_old - m_new), jnp.exp2(s - m_new).astype(BF16)

    def body(j, carry):
        m0, m1, acc0, acc1 = carry
        r0 = pl.multiple_of(j * B_TK, B_TK)
        kc = k_ref[pl.ds(r0, B_TK), :]
        m0, a0, p0 = update(kc[:, :LANES], q0, m0)
        m1, a1, p1 = update(kc[:, LANES:], q1, m1)
        vt = vt_ref[j]
        acc0 = acc0 * a0 + jnp.dot(jnp.where(row_lo, vt, ones0), p0, preferred_element_type=F32)
        acc1 = acc1 * a1 + jnp.dot(jnp.where(row_lo, ones1, vt), p1, preferred_element_type=F32)
        return m0, m1, acc0, acc1

    vec = jnp.full((1, tq), NEG_INF, F32)
    zero = jnp.zeros((LANES, tq), F32)
    _, _, acc0, acc1 = lax.fori_loop(0, n_k, body, (vec, vec, zero, zero), unroll=unroll)
    out = jnp.where(row_lo, acc0 * (1.0 / acc0[LANES // 2:LANES // 2 + 1]), acc1 * (1.0 / acc1[0:1]))
    o_ref[...] = out.T.astype(BF16)


def _attn_b(q, k, vt, seq_len, tq=1024, unroll=4):
    n = q.shape[0]
    tq = min(tq, seq_len)
    n_k = seq_len // B_TK
    unroll = min(unroll, n_k)
    assert seq_len % tq == 0 and seq_len % B_TK == 0 and n % seq_len == 0 and n_k % unroll == 0
    nq = seq_len // tq
    nv = vt.shape[1]
    return pl.pallas_call(
        functools.partial(_attn_b_kernel, tq=tq, n_k=n_k, unroll=unroll),
        out_shape=jax.ShapeDtypeStruct((n, nv), BF16),
        grid=(n // seq_len, B_HEADS // 2, nq),
        in_specs=[pl.BlockSpec((tq, 2 * LANES), lambda b, p, i: (b * nq + i, p)),
                  pl.BlockSpec((seq_len, 2 * LANES), lambda b, p, i: (b, p)),
                  pl.BlockSpec((n_k, LANES, B_TK), lambda b, p, i: (b, p, 0))],
        out_specs=pl.BlockSpec((tq, LANES), lambda b, p, i: (b * nq + i, p)),
        compiler_params=_cparams(3),
    )(q, k, vt)


def _attn_c_kernel(q_ref, k0, k1, k2, k3, k4, v0, v1, v2, v3, v4, tab_ref, o_ref):
    lo = _low_half()
    nq = q_ref.shape[2]
    ones = jnp.ones((k0.shape[2] * 5, LANES), BF16)
    for i in range(q_ref.shape[1]):
        for p in range(C_HEADS // 2):
            ps = slice(p * LANES, (p + 1) * LANES)
            kc = jnp.concatenate([r[0, i, :, ps] for r in (k0, k1, k2, k3, k4)], axis=0)
            vc = jnp.concatenate([r[0, i, :, ps] for r in (v0, v1, v2, v3, v4)], axis=0)
            s = _dot_nt(_split_pair(q_ref[0, i, :, ps], lo), kc)
            probs = []
            for e in range(2):
                sh = s[e * nq:(e + 1) * nq] + tab_ref[0, 2 * p + e]
                probs.append(jnp.exp2(sh - jnp.max(sh, axis=-1, keepdims=True)).astype(BF16))
            r = jnp.dot(jnp.concatenate(probs, axis=0), jnp.concatenate([vc, ones], axis=1),
                        preferred_element_type=F32)
            r = r[:, :LANES] * (1.0 / r[:, LANES:])
            o_ref[0, i, :, ps] = jnp.where(lo, r[:nq], r[nq:]).astype(BF16)


def _c_window_start(rb, rows):
    return jnp.clip(rb - 2, 0, rows // 2 - C_KROWS // 2)


def _c_bias_tables(rpb, rows):
    nrb = rows // C_QROWS
    col = np.arange(GRID_W)
    cs = np.clip(col - C_WIN_COLS // 2, 0, GRID_W - C_WIN_COLS)
    col_ok = (col[None, :] >= cs[:, None]) & (col[None, :] < cs[:, None] + C_WIN_COLS)
    pad = GRID_W - C_WIN_COLS
    padded = jnp.pad(rpb, ((0, 0), (0, 0), (pad, pad)))
    e = jnp.stack([padded[:, :, GRID_W - 1 - qc:2 * GRID_W - 1 - qc] for qc in range(GRID_W)], axis=2) * LOG2_E
    e = jnp.where(jnp.asarray(col_ok)[None, None], e, NEG_INF)
    masked = jnp.full((C_HEADS, GRID_W, GRID_W), NEG_INF, F32)
    tabs = []
    for rb in (0, 1, 2, nrb - 2, nrb - 1):
        w0 = 2 * int(np.clip(rb - 2, 0, rows // 2 - C_KROWS // 2))
        per_qrow = []
        for qrow in range(rb * C_QROWS, (rb + 1) * C_QROWS):
            rs = int(np.clip(qrow - C_WIN_ROWS // 2, 0, rows - C_WIN_ROWS))
            planes = [e[:, krow - qrow + C_WIN_ROWS - 1] if rs <= krow < rs + C_WIN_ROWS else masked
                      for krow in range(w0, w0 + C_KROWS)]
            per_qrow.append(jnp.stack(planes, axis=2))
        tabs.append(jnp.stack(per_qrow, axis=1).reshape(C_HEADS, C_QROWS * GRID_W, C_KROWS * GRID_W))
    return jnp.stack(tabs)


def _attn_c(qkv, rpb, seq_len, group=2):
    n = qkv.shape[0]
    rows = seq_len // GRID_W
    nrb = rows // C_QROWS
    nb = n // seq_len
    assert seq_len % GRID_W == 0 and rows % 2 == 0 and rows >= C_KROWS + 2 and nrb >= 6 and nb % group == 0
    tq = C_QROWS * GRID_W
    nkb = C_KROWS * GRID_W // tq
    nh = C_HEADS * C_HEAD_DIM
    tables = _c_bias_tables(rpb.astype(F32), rows)
    qkv4 = qkv.reshape(nb // group, group, seq_len, qkv.shape[1])

    def kv(col, j):
        return pl.BlockSpec((1, group, tq, nh), lambda rb, b: (b, 0, _c_window_start(rb, rows) + j, col))

    def pattern(rb, b):
        pat = jnp.where(rb < 2, rb, jnp.where(rb >= nrb - 2, rb - (nrb - 2) + 3, 2))
        return (pat, 0, 0, 0)

    out = pl.pallas_call(
        _attn_c_kernel,
        out_shape=jax.ShapeDtypeStruct((nb // group, group, seq_len, nh), BF16),
        grid=(nrb, nb // group),
        in_specs=[pl.BlockSpec((1, group, tq, nh), lambda rb, b: (b, 0, rb, 0))]
                 + [kv(1, j) for j in range(nkb)] + [kv(2, j) for j in range(nkb)]
                 + [pl.BlockSpec((1, C_HEADS, tq, C_KROWS * GRID_W), pattern)],
        out_specs=pl.BlockSpec((1, group, tq, nh), lambda rb, b: (b, 0, rb, 0)),
        compiler_params=_cparams(2),
    )(*([qkv4] * (1 + 2 * nkb)), tables)
    return out.reshape(n, nh)


def _ffn_kernel(*refs, tm, tiles_per_seq, final):
    if final:
        (xm_ref, xp_ref, xn_ref, om_ref, op_ref, on_ref, wo_ref, g_ref, win_ref, cw_ref, wout_ref, fg_ref,
         y_ref, o_scr, h_scr, a_scr) = refs
    else:
        (xm_ref, xp_ref, xn_ref, om_ref, op_ref, on_ref, wo_ref, g_ref, win_ref, cw_ref, wout_ref,
         y_ref, o_scr, h_scr, a_scr) = refs
    t = pl.program_id(0) % tiles_per_seq
    g = g_ref[...]
    nxt = lax.broadcasted_iota(jnp.int32, (BF16_ROWS, 1), 0) < BF16_ROWS // 2
    o_scr[0:tm] = om_ref[...]
    o_scr[tm:] = jnp.where(nxt, on_ref[...].astype(F32), op_ref[...].astype(F32)).astype(BF16)
    mixed = jnp.dot(o_scr[...], wo_ref[...], preferred_element_type=F32)
    x1 = xm_ref[...] + mixed[:tm]
    x1_halo = jnp.where(nxt, xn_ref[...], xp_ref[...]) + mixed[tm:]
    y_ref[...] = x1
    h_scr[0:tm] = _rms(x1, g).astype(BF16)
    inside = jnp.where(nxt, jnp.where(t == tiles_per_seq - 1, 0.0, 1.0), jnp.where(t == 0, 0.0, 1.0))
    h_scr[tm:] = jnp.where(inside > 0.0, _rms(x1_halo, g), 0.0).astype(BF16)
    m = tm + BF16_ROWS
    off = 0
    for fc in FFN_CHUNKS:
        cols = slice(2 * off, 2 * (off + fc))
        u = jnp.dot(h_scr[...], win_ref[:, cols], preferred_element_type=F32)
        cw = cw_ref[:, cols]
        z = (cw[3:4] + pltpu.roll(u, 1, 0)[:tm] * cw[0:1] + u[:tm] * cw[1:2]
             + pltpu.roll(u, m - 1, 0)[:tm] * cw[2:3])
        gate, val = z[:, :fc], z[:, fc:]
        a_scr[:, off:off + fc] = (gate * (1.0 / (1.0 + jnp.exp(-gate))) * val).astype(BF16)
        off += fc
    y = y_ref[...] + jnp.dot(a_scr[...], wout_ref[...], preferred_element_type=F32)
    if final:
        y = _rms(y, fg_ref[...])
    y_ref[...] = y


def _mix_ffn(x, o, w_o, gain, w_in, cw, w_out, seq_len, final_gain=None, tm=512):
    n, d = x.shape
    assert n % tm == 0 and seq_len % tm == 0 and tm % BF16_ROWS == 0 and o.shape == (n, w_o.shape[0])
    tps = seq_len // tm
    r = tm // BF16_ROWS
    nblk = n // BF16_ROWS
    final = final_gain is not None

    def tile_specs(width):
        return [pl.BlockSpec((tm, width), lambda i: (i, 0)),
                pl.BlockSpec((BF16_ROWS, width), lambda i: (jnp.maximum(i * r - 1, 0), 0)),
                pl.BlockSpec((BF16_ROWS, width), lambda i: (jnp.minimum((i + 1) * r, nblk - 1), 0))]

    in_specs = tile_specs(d) + tile_specs(o.shape[1]) + [
        _resident(w_o.shape), _resident((1, d)), _resident(w_in.shape), _resident(cw.shape), _resident(w_out.shape)]
    args = [x, x, x, o, o, o, w_o, gain.reshape(1, d), w_in, cw, w_out]
    if final:
        in_specs.append(_resident((1, d)))
        args.append(final_gain.reshape(1, d))
    return pl.pallas_call(
        functools.partial(_ffn_kernel, tm=tm, tiles_per_seq=tps, final=final),
        out_shape=jax.ShapeDtypeStruct((n, d), F32),
        grid=(n // tm,),
        in_specs=in_specs,
        out_specs=pl.BlockSpec((tm, d), lambda i: (i, 0)),
        scratch_shapes=[pltpu.VMEM((tm + BF16_ROWS, o.shape[1]), BF16), pltpu.VMEM((tm + BF16_ROWS, d), BF16),
                        pltpu.VMEM((tm, w_out.shape[0]), BF16)],
        compiler_params=_cparams(1),
    )(*args)


def _a_head_order():
    order = []
    for p in range(A_KV_HEADS // 2):
        for g in range(A_GROUP):
            order += [(2 * p) * A_GROUP + g, (2 * p + 1) * A_GROUP + g]
    return np.asarray(order)


def _prep_a(w_qkv, w_o):
    cols = (_a_head_order()[:, None] * A_HEAD_DIM + np.arange(A_HEAD_DIM)[None, :]).reshape(-1)
    wq = w_qkv[:, :A_Q_COLS][:, cols] * (A_HEAD_DIM ** -0.5)
    w = jnp.concatenate([wq, w_qkv[:, A_Q_COLS:]], axis=1).astype(BF16)
    return w, w_o[cols, :].astype(BF16)


def _prep_b(w_in, w_uq, w_ukv):
    d = w_in.shape[0]
    nl = B_Q_LORA + B_KV_LORA
    win = jnp.concatenate([w_in[:, :nl], jnp.zeros((d, B_NOPE), F32), w_in[:, nl:],
                           jnp.zeros((d, LANES - B_NOPE - B_ROPE), F32)], axis=1).astype(BF16)
    uq = w_uq.reshape(B_Q_LORA, B_HEADS, B_NOPE + B_ROPE)
    uq = jnp.pad(uq, ((0, 0), (0, 0), (0, LANES - B_NOPE - B_ROPE))).reshape(B_Q_LORA, B_HEADS * LANES)
    ukv = w_ukv.reshape(B_KV_LORA, B_HEADS, B_NOPE + B_V)
    uk = jnp.pad(ukv[:, :, :B_NOPE], ((0, 0), (0, 0), (0, LANES - B_NOPE))).reshape(B_KV_LORA, B_HEADS * LANES)
    uv = ukv[:, :, B_NOPE:].reshape(B_KV_LORA, B_HEADS * B_V)
    return win, uq.astype(BF16), uk.astype(BF16), uv.T.astype(BF16)


def _prep_c(w_qkv):
    nh = C_HEADS * C_HEAD_DIM
    return jnp.concatenate([w_qkv[:, :nh] * (C_HEAD_DIM ** -0.5), w_qkv[:, nh:]], axis=1).astype(BF16)


def _prep_ffn(w_in, conv_w, conv_b, w_out):
    assert sum(FFN_CHUNKS) == FFN_DIM

    def chunked(a):
        parts, off = [], 0
        for fc in FFN_CHUNKS:
            parts += [a[:, off:off + fc], a[:, FFN_DIM + off:FFN_DIM + off + fc]]
            off += fc
        return jnp.concatenate(parts, axis=1)

    win = chunked(w_in).astype(BF16)
    cw = chunked(jnp.concatenate([conv_w, conv_b[None, :]], axis=0)).astype(F32)
    return win, cw, w_out.astype(BF16)


def _rope_tables_a(seq_len):
    half = A_HEAD_DIM // 2
    inv = ROPE_THETA ** (-jnp.arange(0, A_HEAD_DIM, 2, dtype=F32) / A_HEAD_DIM)
    ang = jnp.arange(seq_len).astype(F32)[:, None] * inv[None, :]
    cos, sin = jnp.cos(ang), jnp.sin(ang)
    reps = LANES // A_HEAD_DIM
    return jnp.tile(jnp.concatenate([cos, cos], axis=1), (1, reps)), jnp.tile(jnp.concatenate([-sin, sin], axis=1), (1, reps))


def _rope_tables_b(seq_len):
    half = B_ROPE // 2
    inv = ROPE_THETA ** (-jnp.arange(0, B_ROPE, 2, dtype=F32) / B_ROPE)
    ang = jnp.arange(seq_len).astype(F32)[:, None] * inv[None, :]
    cos, sin = jnp.cos(ang), jnp.sin(ang)
    zeros = lambda c: jnp.zeros((seq_len, c), F32)
    ones = lambda c: jnp.ones((seq_len, c), F32)
    tail = LANES - B_NOPE - B_ROPE
    cos_t = jnp.concatenate([ones(B_NOPE), cos, cos, ones(tail)], axis=1)
    sa = jnp.concatenate([zeros(B_NOPE), -sin, zeros(half), zeros(tail)], axis=1)
    sb = jnp.concatenate([zeros(B_NOPE), zeros(half), sin, zeros(tail)], axis=1)
    return cos_t, sa, sb


def _trunk(x, seq_len, norm_mix, norm_ffn, norm_final, layers, ffns, a_sink, b_norms, c_rpb):
    depth = len(layers)
    rope_a = _rope_tables_a(seq_len)
    rope_b = _rope_tables_b(seq_len)
    for i in range(depth):
        kind, j = i % 3, i // 3
        if kind == 0:
            w_qkv, w_o = layers[i]
            qkv = _norm_proj(x, norm_mix[i], w_qkv, seq_len, A_Q_COLS, rope=rope_a, rope_cols=A_Q_COLS + A_KV_COLS)
            o = _attn_a(qkv, a_sink[j], seq_len)
        elif kind == 1:
            w_in, w_uq, w_uk, w_vt, w_o = layers[i]
            q, k, vt = _proj_b(x, norm_mix[i], w_in, b_norms[0][j], b_norms[1][j], w_uq, w_uk, w_vt, rope_b, seq_len)
            o = _attn_b(q, k, vt, seq_len)
        else:
            w_qkv, w_o = layers[i]
            qkv = _norm_proj(x, norm_mix[i], w_qkv, seq_len, C_HEADS * C_HEAD_DIM)
            o = _attn_c(qkv, c_rpb[j], seq_len)
        fin = norm_final if i == depth - 1 else None
        x = _mix_ffn(x, o, w_o, norm_ffn[i], *ffns[i], seq_len, final_gain=fin)
    return x


def kernel(x_prompt, x_sample, norm_mix, norm_ffn, norm_final, a_w_qkv, a_w_o, a_sink, b_w_in, b_q_norm, b_kv_norm, b_w_uq, b_w_ukv, b_w_o, c_w_qkv, c_rpb, c_w_o, f_w_in, f_conv_w, f_conv_b, f_w_out):
    depth = norm_mix.shape[0]
    layers, ffns = [], []
    for i in range(depth):
        kind, j = i % 3, i // 3
        if kind == 0:
            layers.append(_prep_a(a_w_qkv[j], a_w_o[j]))
        elif kind == 1:
            layers.append(_prep_b(b_w_in[j], b_w_uq[j], b_w_ukv[j]) + (b_w_o[j].astype(BF16),))
        else:
            layers.append((_prep_c(c_w_qkv[j]), c_w_o[j].astype(BF16)))
        ffns.append(_prep_ffn(f_w_in[i], f_conv_w[i], f_conv_b[i], f_w_out[i]))
    outs = []
    for x in (x_prompt, x_sample):
        b, t, d = x.shape
        y = _trunk(x.reshape(b * t, d), t, norm_mix, norm_ffn, norm_final, layers, ffns, a_sink,
                   (b_q_norm, b_kv_norm), c_rpb)
        outs.append(y.reshape(b, t, d))
    return tuple(outs)
```
